```python
import jax, jax.numpy as jnp
from jax import lax
import numpy as np

D_MODEL = 1024
BATCH = 32
SEQ = 2048
DEPTH = 2
DEC_BATCH = 32
DEC_SEQ = 64
PAST_LEN = 1024

CHUNK = 64
Q_BLOCK = 128
EPS = 1e-6

SSD_HEADS = 8
SSD_HEAD_DIM = 64
SSD_WIDTH = SSD_HEADS * SSD_HEAD_DIM
SSD_GROUPS = 2
D_STATE = 128
CONV_W = 4
CONV_DIM = SSD_WIDTH + 2 * SSD_GROUPS * D_STATE

MLA_HEADS = 8
QK_NOPE = 64
QK_ROPE = 32
V_HEAD = 64
Q_LORA = 256
KV_LORA = 128
MLA_WIDTH = MLA_HEADS * V_HEAD
ROPE_BASE = 10000.0

MIX_WIDTH = SSD_WIDTH + MLA_WIDTH
S_Z = SSD_WIDTH
S_XBC = S_Z + CONV_DIM
S_DT = S_XBC + SSD_HEADS
S_Q = S_DT + Q_LORA
IN_PROJ = S_Q + KV_LORA + QK_ROPE

D_FF = -(-(8 * D_MODEL) // (3 * 256)) * 256

kernel_name = "hybrid_ssd_mla_streaming_step"


def rmsnorm(x, w):
    xf = x.astype(jnp.float32)
    y = xf * lax.rsqrt(jnp.mean(xf * xf, axis=-1, keepdims=True) + EPS)
    return (y * w.astype(jnp.float32)).astype(x.dtype)


def rope_tables(pos):
    inv = 1.0 / (ROPE_BASE ** (jnp.arange(0, QK_ROPE, 2, dtype=jnp.float32) / QK_ROPE))
    ang = pos.astype(jnp.float32)[:, None] * inv[None, :]
    return jnp.cos(ang), jnp.sin(ang)


def apply_rope(x, cos, sin):
    x1, x2 = jnp.split(x.astype(jnp.float32), 2, axis=-1)
    return jnp.concatenate([x1 * cos - x2 * sin, x1 * sin + x2 * cos], axis=-1).astype(x.dtype)


def causal_conv(u, buf, w, b):
    T = u.shape[1]
    up = jnp.concatenate([buf.astype(u.dtype), u], axis=1)
    y = b
    for k in range(CONV_W):
        y = y + up[:, k:k + T] * w[k]
    return jax.nn.silu(y), up[:, -(CONV_W - 1):]


def ssd_scan(xh, dt, A, Bm, Cm, h0):
    b, T = xh.shape[:2]
    L = min(CHUNK, T)
    nc = T // L
    hg = SSD_HEADS // SSD_GROUPS
    f32 = jnp.float32

    def to_chunks(a):
        return jnp.moveaxis(a.reshape((b, nc, L) + a.shape[2:]), 1, 0)

    xc = to_chunks(xh.astype(f32).reshape(b, T, SSD_GROUPS, hg, SSD_HEAD_DIM))
    dtc = to_chunks(dt.reshape(b, T, SSD_GROUPS, hg))
    Bc = to_chunks(Bm.astype(f32))
    Cc = to_chunks(Cm.astype(f32))
    causal = jnp.tril(jnp.ones((L, L), dtype=bool))[None, :, :, None, None]
    A_g = A.reshape(SSD_GROUPS, hg)

    def step(h, inp):
        x, d, Bk, Ck = inp
        cum = jnp.cumsum(d * A_g, axis=1)
        seg = cum[:, :, None] - cum[:, None, :]
        decay = jnp.where(causal, jnp.exp(jnp.where(causal, seg, 0.0)), 0.0)
        cb = jnp.einsum('blgn,bsgn->blsg', Ck, Bk)
        wts = decay * cb[..., None] * d[:, None]
        y = jnp.einsum('blsgh,bsghp->blghp', wts, x)
        y = y + jnp.einsum('blgn,bghpn,blgh->blghp', Ck, h, jnp.exp(cum))
        last = cum[:, -1]
        w_end = jnp.exp(last[:, None] - cum) * d
        h_new = jnp.exp(last)[..., None, None] * h + jnp.einsum('blgh,blghp,blgn->bghpn', w_end, x, Bk)
        return h_new, y

    h_init = h0.astype(f32).reshape(b, SSD_GROUPS, hg, SSD_HEAD_DIM, D_STATE)
    hT, ys = lax.scan(step, h_init, (xc, dtc, Bc, Cc))
    y = jnp.moveaxis(ys, 0, 1).reshape(b, T, SSD_HEADS, SSD_HEAD_DIM)
    return y, hT.reshape(b, SSD_HEADS, SSD_HEAD_DIM, D_STATE).astype(h0.dtype)


def mla_attend(q_lat, q_rope, q_pos, keys_c, keys_r, k_pos):
    b, T = q_lat.shape[:2]
    qb = min(Q_BLOCK, T)
    nb = T // qb
    scale = (QK_NOPE + QK_ROPE) ** -0.5
    k_chunk = k_pos // CHUNK

    def block(args):
        ql, qr, qp = args
        s = (jnp.einsum('bqhl,bkl->bhqk', ql, keys_c, preferred_element_type=jnp.float32)
             + jnp.einsum('bqhr,bkr->bhqk', qr, keys_r, preferred_element_type=jnp.float32))
        mask = k_chunk[None, :] <= (qp // CHUNK)[:, None]
        s = jnp.where(mask, s * scale, -jnp.inf)
        pr = jax.nn.softmax(s, axis=-1).astype(keys_c.dtype)
        return jnp.einsum('bhqk,bkl->bqhl', pr, keys_c)

    def split(a):
        return jnp.moveaxis(a.reshape((b, nb, qb) + a.shape[2:]), 1, 0)

    out = lax.map(block, (split(q_lat), split(q_rope), q_pos.reshape(nb, qb)))
    return jnp.moveaxis(out, 0, 1).reshape(b, T, MLA_HEADS, KV_LORA)


def hybrid_layer(x, ckv_past, krope_past, ssm_h0, conv_buf, p):
    b, T, _ = x.shape
    P = ckv_past.shape[1]
    h = rmsnorm(x, p['norm_mix'])
    proj = h @ p['w_in']
    z = proj[..., :S_Z]
    xbc = proj[..., S_Z:S_XBC]
    dt_raw = proj[..., S_XBC:S_DT]
    cq = proj[..., S_DT:S_Q]
    ckv_raw = proj[..., S_Q:]

    xbc, conv_new = causal_conv(xbc, conv_buf, p['conv_w'], p['conv_b'])
    xs = xbc[..., :SSD_WIDTH]
    Bm = xbc[..., SSD_WIDTH:SSD_WIDTH + SSD_GROUPS * D_STATE].reshape(b, T, SSD_GROUPS, D_STATE)
    Cm = xbc[..., SSD_WIDTH + SSD_GROUPS * D_STATE:].reshape(b, T, SSD_GROUPS, D_STATE)
    dt = jax.nn.softplus(dt_raw.astype(jnp.float32) + p['dt_bias'].astype(jnp.float32))
    A = -jnp.exp(p['a_log'].astype(jnp.float32))
    xh = xs.reshape(b, T, SSD_HEADS, SSD_HEAD_DIM)
    y, h_new = ssd_scan(xh, dt, A, Bm, Cm, ssm_h0)
    y = y + p['d_skip'].astype(jnp.float32)[:, None] * xh.astype(jnp.float32)
    y = y.reshape(b, T, SSD_WIDTH) * jax.nn.silu(z.astype(jnp.float32))
    y_ssd = rmsnorm(y, p['ssd_norm']).astype(x.dtype)

    q_pos = P + jnp.arange(T)
    cos, sin = rope_tables(q_pos)
    q = (rmsnorm(cq, p['q_norm']) @ p['w_uq']).reshape(b, T, MLA_HEADS, QK_NOPE + QK_ROPE)
    q_nope = q[..., :QK_NOPE]
    q_rope = apply_rope(q[..., QK_NOPE:], cos[:, None], sin[:, None])
    c_kv = rmsnorm(ckv_raw[..., :KV_LORA], p['kv_norm'])
    k_rope = apply_rope(ckv_raw[..., KV_LORA:], cos, sin)
    q_lat = jnp.einsum('bthd,lhd->bthl', q_nope, p['w_uk'])
    keys_c = jnp.concatenate([ckv_past.astype(c_kv.dtype), c_kv], axis=1)
    keys_r = jnp.concatenate([krope_past.astype(k_rope.dtype), k_rope], axis=1)
    o_lat = mla_attend(q_lat, q_rope, q_pos, keys_c, keys_r, jnp.arange(P + T))
    o = jnp.einsum('bthl,lhd->bthd', o_lat, p['w_uv']).reshape(b, T, MLA_WIDTH)

    x = x + jnp.concatenate([y_ssd, o], axis=-1) @ p['w_out']
    hf = rmsnorm(x, p['norm_ffn'])
    x = x + (jax.nn.silu(hf @ p['w_gate']) * (hf @ p['w_up'])) @ p['w_down']
    return x, c_kv, k_rope, h_new, conv_new


def setup_inputs(seed: int = 0) -> dict:
    key = jax.random.key(seed)
    ks = jax.random.split(key, 32)
    f32 = jnp.float32
    nrm = lambda k, shape, s: jax.random.normal(k, shape, f32) * s
    gain = lambda k, shape: 1.0 + 0.02 * jax.random.normal(k, shape, f32)
    dt0 = jnp.exp(jax.random.uniform(ks[20], (DEPTH, SSD_HEADS), f32, np.log(1e-3), np.log(1e-1)))
    return {
        "x_prompt": jax.random.normal(ks[0], (BATCH, SEQ, D_MODEL), f32),
        "x_sample": jax.random.normal(ks[1], (DEC_BATCH, DEC_SEQ, D_MODEL), f32),
        "cache_mla_ckv": jax.random.normal(ks[2], (DEPTH, DEC_BATCH, PAST_LEN, KV_LORA), f32),
        "cache_mla_krope": jax.random.normal(ks[3], (DEPTH, DEC_BATCH, PAST_LEN, QK_ROPE), f32),
        "state_ssm": nrm(ks[4], (DEPTH, DEC_BATCH, SSD_HEADS, SSD_HEAD_DIM, D_STATE), 0.1),
        "state_conv": jax.random.normal(ks[5], (DEPTH, DEC_BATCH, CONV_W - 1, CONV_DIM), f32),
        "w_in": nrm(ks[6], (DEPTH, D_MODEL, IN_PROJ), D_MODEL ** -0.5),
        "w_uq": nrm(ks[7], (DEPTH, Q_LORA, MLA_HEADS * (QK_NOPE + QK_ROPE)), Q_LORA ** -0.5),
        "w_uk": nrm(ks[8], (DEPTH, KV_LORA, MLA_HEADS, QK_NOPE), KV_LORA ** -0.5),
        "w_uv": nrm(ks[9], (DEPTH, KV_LORA, MLA_HEADS, V_HEAD), KV_LORA ** -0.5),
        "w_out": nrm(ks[10], (DEPTH, MIX_WIDTH, D_MODEL), MIX_WIDTH ** -0.5),
        "norm_mix": gain(ks[11], (DEPTH, D_MODEL)),
        "q_norm": gain(ks[12], (DEPTH, Q_LORA)),
        "kv_norm": gain(ks[13], (DEPTH, KV_LORA)),
        "ssd_norm": gain(ks[14], (DEPTH, SSD_WIDTH)),
        "conv_w": nrm(ks[15], (DEPTH, CONV_W, CONV_DIM), CONV_W ** -0.5),
        "conv_b": nrm(ks[16], (DEPTH, CONV_DIM), 0.01),
        "dt_bias": dt0 + jnp.log(-jnp.expm1(-dt0)),
        "a_log": jnp.log(jax.random.uniform(ks[17], (DEPTH, SSD_HEADS), f32, 1.0, 16.0)),
        "d_skip": 1.0 + 0.1 * jax.random.normal(ks[18], (DEPTH, SSD_HEADS), f32),
        "norm_ffn": gain(ks[19], (DEPTH, D_MODEL)),
        "w_gate": nrm(ks[21], (DEPTH, D_MODEL, D_FF), D_MODEL ** -0.5),
        "w_up": nrm(ks[22], (DEPTH, D_MODEL, D_FF), D_MODEL ** -0.5),
        "w_down": nrm(ks[23], (DEPTH, D_FF, D_MODEL), D_FF ** -0.5),
        "norm_final": gain(ks[24], (D_MODEL,)),
    }


def reference(x_prompt, x_sample, cache_mla_ckv, cache_mla_krope, state_ssm, state_conv,
              w_in, w_uq, w_uk, w_uv, w_out, norm_mix, q_norm, kv_norm, ssd_norm,
              conv_w, conv_b, dt_bias, a_log, d_skip, norm_ffn, w_gate, w_up, w_down, norm_final):
    xp, xs = x_prompt, x_sample
    bp = xp.shape[0]
    p_ckv, p_kr, p_ssm, p_conv = [], [], [], []
    s_ckv, s_kr, s_ssm, s_conv = [], [], [], []
    for l in range(DEPTH):
        p = {"w_in": w_in[l], "w_uq": w_uq[l], "w_uk": w_uk[l], "w_uv": w_uv[l], "w_out": w_out[l],
             "norm_mix": norm_mix[l], "q_norm": q_norm[l], "kv_norm": kv_norm[l], "ssd_norm": ssd_norm[l],
             "conv_w": conv_w[l], "conv_b": conv_b[l], "dt_bias": dt_bias[l], "a_log": a_log[l],
             "d_skip": d_skip[l], "norm_ffn": norm_ffn[l], "w_gate": w_gate[l], "w_up": w_up[l],
             "w_down": w_down[l]}
        xp, ckv, kr, hs, cb = hybrid_layer(
            xp,
            jnp.zeros((bp, 0, KV_LORA), xp.dtype),
            jnp.zeros((bp, 0, QK_ROPE), xp.dtype),
            jnp.zeros((bp, SSD_HEADS, SSD_HEAD_DIM, D_STATE), state_ssm.dtype),
            jnp.zeros((bp, CONV_W - 1, CONV_DIM), xp.dtype),
            p)
        p_ckv.append(ckv); p_kr.append(kr); p_ssm.append(hs); p_conv.append(cb)
        xs, ckv, kr, hs, cb = hybrid_layer(xs, cache_mla_ckv[l], cache_mla_krope[l], state_ssm[l], state_conv[l], p)
        s_ckv.append(ckv); s_kr.append(kr); s_ssm.append(hs); s_conv.append(cb)
    y_prompt = rmsnorm(xp, norm_final)
    y_sample = rmsnorm(xs, norm_final)
    return (y_prompt, y_sample,
            jnp.stack(p_ckv), jnp.stack(p_kr), jnp.stack(p_ssm), jnp.stack(p_conv),
            jnp.stack(s_ckv), jnp.stack(s_kr), jnp.stack(s_ssm), jnp.stack(s_conv))
```

```python
import functools

import jax
import jax.numpy as jnp
from jax import lax
from jax.experimental import pallas as pl
from jax.experimental.pallas import tpu as pltpu

F32 = jnp.float32
BF16 = jnp.bfloat16

D_MODEL = 1024
CHUNK = 64
EPS = 1e-6
SSD_HEADS = 8
SSD_HEAD_DIM = 64
SSD_WIDTH = SSD_HEADS * SSD_HEAD_DIM
SSD_GROUPS = 2
D_STATE = 128
CONV_W = 4
CONV_DIM = SSD_WIDTH + 2 * SSD_GROUPS * D_STATE
MLA_HEADS = 8
QK_NOPE = 64
QK_ROPE = 32
V_HEAD = 64
Q_LORA = 256
KV_LORA = 128
MLA_WIDTH = MLA_HEADS * V_HEAD
ROPE_BASE = 10000.0
S_Z = SSD_WIDTH
S_XBC = S_Z + CONV_DIM
S_DT = S_XBC + SSD_HEADS
S_Q = S_DT + Q_LORA
D_FF = 2816

LANES = 128
KEY_WIDTH = 2 * LANES
PROJ_WIDTH = 2048
MISC_KR = 32
MISC_KR_SWAPPED = 64
VMEM_LIMIT = 56 * 1024 * 1024

_NT = (((1,), (1,)), ((), ()))
_TN = (((0,), (0,)), ((), ()))


def _dot(a, b):
    return jnp.dot(a, b, preferred_element_type=F32)


def _rms(x, w):
    return x * lax.rsqrt(jnp.mean(x * x, axis=-1, keepdims=True) + EPS) * w


def _silu(x):
    return x * jax.nn.sigmoid(x)


def _split3(x):
    hi = x.astype(BF16)
    r = x - hi.astype(F32)
    mid = r.astype(BF16)
    lo = (r - mid.astype(F32)).astype(BF16)
    return hi, mid, lo


def _in_proj_kernel(x_ref, nw_ref, w_ref, qn_ref, kvn_ref, dtb_ref, ck_ref, sk_ref,
                    z_ref, xbc_ref, cqn_ref, ckv_ref, kr_ref, keys_ref, dt_ref):
    h = _rms(x_ref[...], nw_ref[...]).astype(BF16)
    z_ref[...] = _dot(h, w_ref[:, 0:512])
    xbc_ref[...] = _dot(h, w_ref[:, 512:1536])
    cqn_ref[...] = _rms(_dot(h, w_ref[:, 1536:1792]), qn_ref[...]).astype(BF16)
    tail = _dot(h, w_ref[:, 1792:2048])
    ckv = _rms(tail[:, :LANES], kvn_ref[...])
    ckv_ref[...] = ckv
    misc = tail[:, LANES:]
    kr = (pltpu.roll(misc, LANES - MISC_KR, 1) * ck_ref[...]
          + pltpu.roll(misc, LANES - MISC_KR_SWAPPED, 1) * sk_ref[...])
    kr_ref[...] = kr[:, :QK_ROPE]
    keys_ref[...] = jnp.concatenate([ckv, kr], axis=1).astype(BF16)
    lane = lax.broadcasted_iota(jnp.int32, misc.shape, 1)
    dt_ref[...] = jnp.where(lane < SSD_HEADS, jax.nn.softplus(misc + dtb_ref[...]), 0.0)


def _in_proj(x2d, T, lp, tabs):
    n = x2d.shape[0]
    tm = min(512, n)
    assert n % tm == 0
    if tm <= T:
        assert T % tm == 0
        per_seq = T // tm
        ck, sk = tabs["ck"], tabs["sk"]
        tab_map = lambda i: (i % per_seq, 0)
    else:
        assert tm % T == 0
        ck = jnp.tile(tabs["ck"], (tm // T, 1))
        sk = jnp.tile(tabs["sk"], (tm // T, 1))
        tab_map = lambda i: (0, 0)
    row = lambda i: (i, 0)
    const = lambda i: (0, 0)
    return pl.pallas_call(
        _in_proj_kernel,
        grid=(n // tm,),
        in_specs=[
            pl.BlockSpec((tm, D_MODEL), row),
            pl.BlockSpec((1, D_MODEL), const),
            pl.BlockSpec((D_MODEL, PROJ_WIDTH), const),
            pl.BlockSpec((1, Q_LORA), const),
            pl.BlockSpec((1, KV_LORA), const),
            pl.BlockSpec((1, LANES), const),
            pl.BlockSpec((tm, LANES), tab_map),
            pl.BlockSpec((tm, LANES), tab_map),
        ],
        out_specs=[
            pl.BlockSpec((tm, SSD_WIDTH), row),
            pl.BlockSpec((tm, CONV_DIM), row),
            pl.BlockSpec((tm, Q_LORA), row),
            pl.BlockSpec((tm, KV_LORA), row),
            pl.BlockSpec((tm, QK_ROPE), row),
            pl.BlockSpec((tm, KEY_WIDTH), row),
            pl.BlockSpec((tm, LANES), row),
        ],
        out_shape=[
            jax.ShapeDtypeStruct((n, SSD_WIDTH), F32),
            jax.ShapeDtypeStruct((n, CONV_DIM), F32),
            jax.ShapeDtypeStruct((n, Q_LORA), BF16),
            jax.ShapeDtypeStruct((n, KV_LORA), F32),
            jax.ShapeDtypeStruct((n, QK_ROPE), F32),
            jax.ShapeDtypeStruct((n, KEY_WIDTH), BF16),
            jax.ShapeDtypeStruct((n, LANES), F32),
        ],
        compiler_params=pltpu.CompilerParams(
            dimension_semantics=("arbitrary",), vmem_limit_bytes=VMEM_LIMIT),
        name="in_proj",
    )(x2d, lp["norm_mix"], lp["w_in"], lp["q_norm"], lp["kv_norm"], lp["dt_bias"], ck, sk)


def _ssd_kernel(xbc_ref, z_ref, dt_ref, h0_ref, cb0_ref, cw_ref, cbias_ref, alog_ref, dskip_ref, nw_ref,
                y_ref, hout_ref, cout_ref, st_ref, ubuf_ref, act_ref, *, ts):
    j = pl.program_id(1)
    pad = 8

    @pl.when(j == 0)
    def _():
        st_ref[...] = h0_ref[...]
        ubuf_ref[pad - (CONV_W - 1):pad, :] = cb0_ref[...]

    ubuf_ref[pad:pad + ts, :] = xbc_ref[...]
    yc = cbias_ref[...]
    for k in range(CONV_W):
        off = pad - (CONV_W - 1) + k
        yc = yc + ubuf_ref[off:off + ts, :] * cw_ref[k:k + 1, :]
    act_ref[...] = _silu(yc)
    tail = ubuf_ref[pad + ts - (CONV_W - 1):pad + ts, :]
    ubuf_ref[pad - (CONV_W - 1):pad, :] = tail
    cout_ref[...] = tail

    L = CHUNK
    lane = lax.broadcasted_iota(jnp.int32, (L, LANES), 1)
    sub = lax.broadcasted_iota(jnp.int32, (L, LANES), 0)
    left = lane < L
    tril2 = (lane % L) <= sub
    r64 = lax.broadcasted_iota(jnp.int32, (L, L), 0)
    c64 = lax.broadcasted_iota(jnp.int32, (L, L), 1)
    tril = (c64 <= r64).astype(BF16)
    sel8 = (lax.broadcasted_iota(jnp.int32, (8, LANES), 0)
            == lax.broadcasted_iota(jnp.int32, (8, LANES), 1)).astype(BF16)
    head_lane = lax.broadcasted_iota(jnp.int32, (1, LANES), 1) < SSD_HEADS
    a_neg = jnp.where(head_lane, -jnp.exp(alog_ref[...]), 0.0)

    def pair_rows(v):
        stacked = jnp.concatenate([v, pltpu.roll(v, LANES - 1, 1)], axis=0)
        out = jnp.zeros((8, LANES), F32)
        for part in _split3(stacked):
            out = out + lax.dot_general(sel8, part, _NT, preferred_element_type=F32)
        return out

    def pair_cols(v, k):
        return jnp.where(left, v[:, 2 * k:2 * k + 1], v[:, 2 * k + 1:2 * k + 2])

    def chunk(c, carry):
        r0 = pl.multiple_of(c * L, L)
        dtc = dt_ref[pl.ds(r0, L), :]
        a = dtc * a_neg
        cum = jnp.zeros((L, LANES), F32)
        for part in _split3(a):
            cum = cum + _dot(tril, part)
        cum_rows = pair_rows(cum)
        dt_rows = pair_rows(dtc)
        last = cum[L - 1:L, :]
        exp_last = jnp.exp(last)
        xs = act_ref[pl.ds(r0, L), 0:SSD_WIDTH]
        ys = []
        for g in range(SSD_GROUPS):
            b_g = act_ref[pl.ds(r0, L), SSD_WIDTH + g * D_STATE:SSD_WIDTH + (g + 1) * D_STATE].astype(BF16)
            c_g = act_ref[pl.ds(r0, L),
                          SSD_WIDTH + (SSD_GROUPS + g) * D_STATE:SSD_WIDTH + (SSD_GROUPS + g + 1) * D_STATE].astype(BF16)
            cb2 = lax.dot_general(c_g, jnp.concatenate([b_g, b_g], axis=0), _NT, preferred_element_type=F32)
            for kk in range(SSD_HEADS // SSD_GROUPS // 2):
                k = g * (SSD_HEADS // SSD_GROUPS // 2) + kk
                ccol = pair_cols(cum, k)
                dtcol = pair_cols(dtc, k)
                crow = cum_rows[2 * k:2 * k + 1, :]
                dtrow = dt_rows[2 * k:2 * k + 1, :]
                decay = jnp.exp(jnp.where(tril2, ccol - crow, -jnp.inf))
                w2 = (decay * cb2 * dtrow).astype(BF16)
                xp = xs[:, 2 * k * SSD_HEAD_DIM:(2 * k + 2) * SSD_HEAD_DIM]
                xbd = jnp.concatenate([jnp.where(left, xp, 0.0), jnp.where(left, 0.0, xp)], axis=0).astype(BF16)
                st = st_ref[2 * k * SSD_HEAD_DIM:(2 * k + 2) * SSD_HEAD_DIM, :]
                y = _dot(w2, xbd) + jnp.exp(ccol) * lax.dot_general(
                    c_g, st.astype(BF16), _NT, preferred_element_type=F32)
                ys.append(y)
                last2 = jnp.where(left[0:1, :], last[:, 2 * k:2 * k + 1], last[:, 2 * k + 1:2 * k + 2])
                xw = (xp * (jnp.exp(last2 - ccol) * dtcol)).astype(BF16)
                upd = lax.dot_general(xw, b_g, _TN, preferred_element_type=F32)
                dec2 = jnp.concatenate(
                    [jnp.broadcast_to(exp_last[:, 2 * k:2 * k + 1], (SSD_HEAD_DIM, D_STATE)),
                     jnp.broadcast_to(exp_last[:, 2 * k + 1:2 * k + 2], (SSD_HEAD_DIM, D_STATE))], axis=0)
                st_ref[2 * k * SSD_HEAD_DIM:(2 * k + 2) * SSD_HEAD_DIM, :] = dec2 * st + upd
        y = jnp.concatenate(ys, axis=1) + dskip_ref[...] * xs
        y = y * _silu(z_ref[pl.ds(r0, L), :])
        y_ref[pl.ds(r0, L), :] = _rms(y, nw_ref[...]).astype(BF16)
        return carry

    lax.fori_loop(0, ts // L, chunk, 0)

    @pl.when(j == pl.num_programs(1) - 1)
    def _():
        hout_ref[...] = st_ref[...]


def _ssd(xbc, z, dt, h0, conv0, B, T, lp):
    ts = min(256, T)
    assert T % ts == 0 and ts % CHUNK == 0 and T >= CONV_W - 1
    nt = T // ts
    row = lambda b, j: (b * nt + j, 0)
    const = lambda b, j: (0, 0)
    per_b = lambda b, j: (b, 0, 0)
    state_rows = SSD_HEADS * SSD_HEAD_DIM
    return pl.pallas_call(
        functools.partial(_ssd_kernel, ts=ts),
        grid=(B, nt),
        in_specs=[
            pl.BlockSpec((ts, CONV_DIM), row),
            pl.BlockSpec((ts, SSD_WIDTH), row),
            pl.BlockSpec((ts, LANES), row),
            pl.BlockSpec((None, state_rows, D_STATE), per_b),
            pl.BlockSpec((None, CONV_W - 1, CONV_DIM), per_b),
            pl.BlockSpec((CONV_W, CONV_DIM), const),
            pl.BlockSpec((1, CONV_DIM), const),
            pl.BlockSpec((1, LANES), const),
            pl.BlockSpec((1, SSD_WIDTH), const),
            pl.BlockSpec((1, SSD_WIDTH), const),
        ],
        out_specs=[
            pl.BlockSpec((ts, SSD_WIDTH), row),
            pl.BlockSpec((None, state_rows, D_STATE), per_b),
            pl.BlockSpec((None, CONV_W - 1, CONV_DIM), per_b),
        ],
        out_shape=[
            jax.ShapeDtypeStruct((B * T, SSD_WIDTH), BF16),
            jax.ShapeDtypeStruct((B, state_rows, D_STATE), F32),
            jax.ShapeDtypeStruct((B, CONV_W - 1, CONV_DIM), F32),
        ],
        scratch_shapes=[
            pltpu.VMEM((state_rows, D_STATE), F32),
            pltpu.VMEM((ts + 8, CONV_DIM), F32),
            pltpu.VMEM((ts, CONV_DIM), F32),
        ],
        compiler_params=pltpu.CompilerParams(
            dimension_semantics=("arbitrary", "arbitrary"), vmem_limit_bytes=VMEM_LIMIT),
        name="ssd",
    )(xbc, z, dt, h0, conv0, lp["conv_w"], lp["conv_b"], lp["a_log"], lp["d_skip"], lp["ssd_norm"])


def _attn_kernel(cqn_ref, keys_ref, wq_ref, wuk_ref, wuv_ref, cq_ref, sq_ref, o_ref,
                 q_scr, m_scr, l_scr, acc_scr, *, tq, tk, past, n_masked):
    i = pl.program_id(1)
    rows = MLA_HEADS * tq
    qall = _dot(cqn_ref[...], wq_ref[...])
    for h in range(MLA_HEADS):
        t = qall[:, h * LANES:(h + 1) * LANES]
        r = t * cq_ref[...] + pltpu.roll(t, LANES - QK_ROPE, 1) * sq_ref[...]
        q_scr[h * tq:(h + 1) * tq, :] = _dot(r.astype(BF16), wuk_ref[h]).astype(BF16)
    m_scr[...] = jnp.full(m_scr.shape, -jnp.inf, F32)
    l_scr[...] = jnp.zeros(l_scr.shape, F32)
    acc_scr[...] = jnp.zeros(acc_scr.shape, F32)

    q_first = past + i * tq
    n_full = q_first // tk

    def step(jt, masked):
        k0 = pl.multiple_of(jt * tk, tk)
        kt = keys_ref[pl.ds(k0, tk), :]
        s = lax.dot_general(q_scr[...], kt, _NT, preferred_element_type=F32)
        if masked:
            qc = (q_first + lax.broadcasted_iota(jnp.int32, (tq, tk), 0)) // CHUNK
            kc = (k0 + lax.broadcasted_iota(jnp.int32, (tq, tk), 1)) // CHUNK
            bias = jnp.where(kc <= qc, 0.0, -jnp.inf).astype(F32)
            s = (s.reshape(MLA_HEADS, tq, tk) + bias[None]).reshape(rows, tk)
        m_prev = m_scr[...]
        m_new = jnp.maximum(m_prev, jnp.max(s, axis=-1, keepdims=True))
        alpha = jnp.exp(m_prev - m_new)
        p = jnp.exp(s - m_new)
        l_scr[...] = alpha * l_scr[...] + jnp.sum(p, axis=-1, keepdims=True)
        acc_scr[...] = alpha * acc_scr[...] + _dot(p.astype(BF16), kt[:, :KV_LORA])
        m_scr[...] = m_new

    def full_step(jt, carry):
        step(jt, False)
        return carry

    lax.fori_loop(0, n_full, full_step, 0)
    for d in range(n_masked):
        step(n_full + d, True)

    o_lat = acc_scr[...] / l_scr[...]
    for k in range(MLA_HEADS // 2):
        pair = jnp.concatenate([o_lat[(2 * k) * tq:(2 * k + 1) * tq, :],
                                o_lat[(2 * k + 1) * tq:(2 * k + 2) * tq, :]], axis=1).astype(BF16)
        o_ref[:, k * LANES:(k + 1) * LANES] = _dot(pair, wuv_ref[k]).astype(BF16)


def _attn(cqn, keys3d, B, T, past, lp, tabs):
    S = keys3d.shape[1]
    assert S == past + T
    if T >= 256:
        tq = tk = 256
        assert T % tq == 0 and past % tk == 0
        n_masked = tq // tk
    else:
        tq, tk = T, S
        n_masked = 1
    nq = T // tq
    rows = MLA_HEADS * tq
    const2 = lambda b, i: (0, 0)
    const3 = lambda b, i: (0, 0, 0)
    return pl.pallas_call(
        functools.partial(_attn_kernel, tq=tq, tk=tk, past=past, n_masked=n_masked),
        grid=(B, nq),
        in_specs=[
            pl.BlockSpec((tq, Q_LORA), lambda b, i: (b * nq + i, 0)),
            pl.BlockSpec((None, S, KEY_WIDTH), lambda b, i: (b, 0, 0)),
            pl.BlockSpec((Q_LORA, MLA_HEADS * LANES), const2),
            pl.BlockSpec((MLA_HEADS, LANES, KEY_WIDTH), const3),
            pl.BlockSpec((MLA_HEADS // 2, 2 * KV_LORA, LANES), const3),
            pl.BlockSpec((tq, LANES), lambda b, i: (i, 0)),
            pl.BlockSpec((tq, LANES), lambda b, i: (i, 0)),
        ],
        out_specs=pl.BlockSpec((tq, MLA_WIDTH), lambda b, i: (b * nq + i, 0)),
        out_shape=jax.ShapeDtypeStruct((B * T, MLA_WIDTH), BF16),
        scratch_shapes=[
            pltpu.VMEM((rows, KEY_WIDTH), BF16),
            pltpu.VMEM((rows, 1), F32),
            pltpu.VMEM((rows, 1), F32),
            pltpu.VMEM((rows, KV_LORA), F32),
        ],
        compiler_params=pltpu.CompilerParams(
            dimension_semantics=("arbitrary", "arbitrary"), vmem_limit_bytes=VMEM_LIMIT),
        name="attn",
    )(cqn, keys3d, lp["w_q"], lp["w_uk"], lp["w_uv"], tabs["cq"], tabs["sq"])


def _ffn_kernel(x_ref, y_ref, o_ref, woa_ref, wob_ref, nf_ref, wg_ref, wu_ref, wd_ref, nfin_ref,
                xo_ref, *, final):
    x1 = x_ref[...] + _dot(y_ref[...], woa_ref[...]) + _dot(o_ref[...], wob_ref[...])
    hf = _rms(x1, nf_ref[...]).astype(BF16)
    g = _dot(hf, wg_ref[...])
    u = _dot(hf, wu_ref[...])
    x2 = x1 + _dot((_silu(g) * u).astype(BF16), wd_ref[...])
    if final:
        x2 = _rms(x2, nfin_ref[...])
    xo_ref[...] = x2


def _ffn(x2d, y, o, lp, norm_final, final):
    n = x2d.shape[0]
    tm = min(256, n)
    assert n % tm == 0
    row = lambda i: (i, 0)
    const = lambda i: (0, 0)
    resident = functools.partial(pl.BlockSpec, index_map=const)
    return pl.pallas_call(
        functools.partial(_ffn_kernel, final=final),
        grid=(n // tm,),
        in_specs=[
            pl.BlockSpec((tm, D_MODEL), row),
            pl.BlockSpec((tm, SSD_WIDTH), row),
            pl.BlockSpec((tm, MLA_WIDTH), row),
            resident((SSD_WIDTH, D_MODEL)),
            resident((MLA_WIDTH, D_MODEL)),
            pl.BlockSpec((1, D_MODEL), const),
            resident((D_MODEL, D_FF)),
            resident((D_MODEL, D_FF)),
            resident((D_FF, D_MODEL)),
            pl.BlockSpec((1, D_MODEL), const),
        ],
        out_specs=pl.BlockSpec((tm, D_MODEL), row),
        out_shape=jax.ShapeDtypeStruct((n, D_MODEL), F32),
        compiler_params=pltpu.CompilerParams(
            dimension_semantics=("arbitrary",), vmem_limit_bytes=VMEM_LIMIT),
        name="ffn",
    )(x2d, y, o, lp["w_out_a"], lp["w_out_b"], lp["norm_ffn"], lp["w_gate"], lp["w_up"], lp["w_down"], norm_final)


def _swap_halves_cols(w):
    half = w.shape[-1] // 2
    return jnp.concatenate([w[..., half:], w[..., :half]], axis=-1)


def _layer_params(l, w_in, w_uq, w_uk, w_uv, w_out, norm_mix, q_norm, kv_norm, ssd_norm,
                  conv_w, conv_b, dt_bias, a_log, d_skip, norm_ffn, w_gate, w_up, w_down):
    wi = w_in[l]
    w_kr = wi[:, S_Q + KV_LORA:]
    misc = jnp.concatenate([
        wi[:, S_XBC:S_DT], jnp.zeros((D_MODEL, MISC_KR - SSD_HEADS), F32),
        w_kr, _swap_halves_cols(w_kr),
        jnp.zeros((D_MODEL, LANES - MISC_KR_SWAPPED - QK_ROPE), F32)], axis=1)
    w_in_r = jnp.concatenate([wi[:, :S_XBC], wi[:, S_DT:S_Q], wi[:, S_Q:S_Q + KV_LORA], misc], axis=1)

    wq = w_uq[l].reshape(Q_LORA, MLA_HEADS, QK_NOPE + QK_ROPE)
    wq_rope = wq[:, :, QK_NOPE:]
    w_q = jnp.concatenate([wq, _swap_halves_cols(wq_rope)], axis=-1).reshape(Q_LORA, MLA_HEADS * LANES)

    uk_t = jnp.transpose(w_uk[l], (1, 2, 0))
    eye = jnp.eye(QK_ROPE, dtype=F32)
    top = jnp.concatenate([uk_t, jnp.zeros((MLA_HEADS, QK_NOPE, KEY_WIDTH - KV_LORA), F32)], axis=2)
    mid = jnp.concatenate([jnp.zeros((QK_ROPE, KV_LORA), F32), eye,
                           jnp.zeros((QK_ROPE, KEY_WIDTH - KV_LORA - QK_ROPE), F32)], axis=1)
    mid = jnp.broadcast_to(mid[None], (MLA_HEADS, QK_ROPE, KEY_WIDTH))
    bot = jnp.zeros((MLA_HEADS, LANES - QK_NOPE - QK_ROPE, KEY_WIDTH), F32)
    w_uk_p = jnp.concatenate([top, mid, bot], axis=1)

    uv = jnp.transpose(w_uv[l], (1, 0, 2)).reshape(MLA_HEADS // 2, 2, KV_LORA, V_HEAD)
    zero = jnp.zeros((MLA_HEADS // 2, KV_LORA, V_HEAD), F32)
    w_uv_p = jnp.concatenate([jnp.concatenate([uv[:, 0], zero], axis=2),
                              jnp.concatenate([zero, uv[:, 1]], axis=2)], axis=1)

    pad_heads = lambda v: jnp.concatenate([v, jnp.zeros((LANES - SSD_HEADS,), F32)])[None]
    return {
        "w_in": w_in_r.astype(BF16), "w_q": w_q.astype(BF16), "w_uk": w_uk_p.astype(BF16),
        "w_uv": w_uv_p.astype(BF16),
        "w_out_a": w_out[l][:SSD_WIDTH].astype(BF16), "w_out_b": w_out[l][SSD_WIDTH:].astype(BF16),
        "w_gate": w_gate[l].astype(BF16), "w_up": w_up[l].astype(BF16), "w_down": w_down[l].astype(BF16),
        "norm_mix": norm_mix[l][None], "q_norm": q_norm[l][None], "kv_norm": kv_norm[l][None],
        "ssd_norm": ssd_norm[l][None], "norm_ffn": norm_ffn[l][None],
        "conv_w": conv_w[l], "conv_b": conv_b[l][None],
        "dt_bias": pad_heads(dt_bias[l]), "a_log": pad_heads(a_log[l]),
        "d_skip": jnp.repeat(d_skip[l], SSD_HEAD_DIM)[None],
    }


def _rope_tables(past, T):
    pos = (past + jnp.arange(T)).astype(F32)
    inv = 1.0 / (ROPE_BASE ** (jnp.arange(0, QK_ROPE, 2, dtype=F32) / QK_ROPE))
    ang = pos[:, None] * inv[None, :]
    cos, sin = jnp.cos(ang), jnp.sin(ang)
    c32 = jnp.concatenate([cos, cos], axis=1)
    s32 = jnp.concatenate([-sin, sin], axis=1)
    scale = (QK_NOPE + QK_ROPE) ** -0.5
    zeros = lambda w: jnp.zeros((T, w), F32)
    return {
        "ck": jnp.concatenate([c32, zeros(LANES - QK_ROPE)], axis=1),
        "sk": jnp.concatenate([s32, zeros(LANES - QK_ROPE)], axis=1),
        "cq": scale * jnp.concatenate([jnp.ones((T, QK_NOPE), F32), c32, zeros(LANES - QK_NOPE - QK_ROPE)], axis=1),
        "sq": scale * jnp.concatenate([zeros(QK_NOPE), s32, zeros(LANES - QK_NOPE - QK_ROPE)], axis=1),
    }


def _hybrid_layer(x2d, B, T, past_keys, h0, conv0, lp, tabs, norm_final, final):
    past = 0 if past_keys is None else past_keys.shape[1]
    z, xbc, cqn, ckv, kr, keys, dt = _in_proj(x2d, T, lp, tabs)
    y, h_new, conv_new = _ssd(xbc, z, dt, h0, conv0, B, T, lp)
    keys3d = keys.reshape(B, T, KEY_WIDTH)
    if past_keys is not None:
        keys3d = jnp.concatenate([past_keys, keys3d], axis=1)
    o = _attn(cqn, keys3d, B, T, past, lp, tabs)
    x_new = _ffn(x2d, y, o, lp, norm_final, final)
    return x_new, ckv, kr, h_new, conv_new


def kernel(x_prompt, x_sample, cache_mla_ckv, cache_mla_krope, state_ssm, state_conv, w_in, w_uq, w_uk, w_uv, w_out, norm_mix, q_norm, kv_norm, ssd_norm, conv_w, conv_b, dt_bias, a_log, d_skip, norm_ffn, w_gate, w_up, w_down, norm_final):
    depth = w_in.shape[0]
    bp, tp, _ = x_prompt.shape
    bs, tsm, _ = x_sample.shape
    past = cache_mla_ckv.shape[2]
    state_rows = SSD_HEADS * SSD_HEAD_DIM

    xp = x_prompt.reshape(bp * tp, D_MODEL)
    xs = x_sample.reshape(bs * tsm, D_MODEL)
    tabs_p = _rope_tables(0, tp)
    tabs_s = _rope_tables(past, tsm)
    zero_state = jnp.zeros((bp, state_rows, D_STATE), F32)
    zero_conv = jnp.zeros((bp, CONV_W - 1, CONV_DIM), F32)
    nfin = norm_final[None]

    outs_p = [[], [], [], []]
    outs_s = [[], [], [], []]
    for l in range(depth):
        lp = _layer_params(l, w_in, w_uq, w_uk, w_uv, w_out, norm_mix, q_norm, kv_norm, ssd_norm,
                           conv_w, conv_b, dt_bias, a_log, d_skip, norm_ffn, w_gate, w_up, w_down)
        final = l == depth - 1
        xp, ckv, kr, hs, cb = _hybrid_layer(xp, bp, tp, None, zero_state, zero_conv, lp, tabs_p, nfin, final)
        for acc, v in zip(outs_p, (ckv.reshape(bp, tp, KV_LORA), kr.reshape(bp, tp, QK_ROPE),
                                   hs.reshape(bp, SSD_HEADS, SSD_HEAD_DIM, D_STATE), cb)):
            acc.append(v)
        past_keys = jnp.concatenate(
            [cache_mla_ckv[l], cache_mla_krope[l],
             jnp.zeros((bs, past, KEY_WIDTH - KV_LORA - QK_ROPE), F32)], axis=-1).astype(BF16)
        xs, ckv, kr, hs, cb = _hybrid_layer(
            xs, bs, tsm, past_keys, state_ssm[l].reshape(bs, state_rows, D_STATE), state_conv[l],
            lp, tabs_s, nfin, final)
        for acc, v in zip(outs_s, (ckv.reshape(bs, tsm, KV_LORA), kr.reshape(bs, tsm, QK_ROPE),
                                   hs.reshape(bs, SSD_HEADS, SSD_HEAD_DIM, D_STATE), cb)):
            acc.append(v)
    return (xp.reshape(bp, tp, D_MODEL), xs.reshape(bs, tsm, D_MODEL),
            *(jnp.stack(v) for v in outs_p), *(jnp.stack(v) for v in outs_s))
```

```python
import functools

import jax
import jax.numpy as jnp
from jax import lax
from jax.experimental import pallas as pl
from jax.experimental.pallas import tpu as pltpu

F32 = jnp.float32
BF16 = jnp.bfloat16

D_MODEL = 1024
CHUNK = 64
EPS = 1e-6
SSD_HEADS = 8
SSD_HEAD_DIM = 64
SSD_WIDTH = SSD_HEADS * SSD_HEAD_DIM
SSD_GROUPS = 2
D_STATE = 128
CONV_W = 4
CONV_DIM = SSD_WIDTH + 2 * SSD_GROUPS * D_STATE
MLA_HEADS = 8
QK_NOPE = 64
QK_ROPE = 32
V_HEAD = 64
Q_LORA = 256
KV_LORA = 128
MLA_WIDTH = MLA_HEADS * V_HEAD
ROPE_BASE = 10000.0
LOG2_E = 1.4426950408889634
S_Z = SSD_WIDTH
S_XBC = S_Z + CONV_DIM
S_DT = S_XBC + SSD_HEADS
S_Q = S_DT + Q_LORA
D_FF = 2816

LANES = 128
KEY_WIDTH = 2 * LANES
PROJ_WIDTH = 2048
MISC_KR = 32
MISC_KR_SWAPPED = 64
VMEM_LIMIT = 56 * 1024 * 1024

_NT = (((1,), (1,)), ((), ()))
_TN = (((0,), (0,)), ((), ()))


def _dot(a, b):
    return jnp.dot(a, b, preferred_element_type=F32)


def _rms(x, w):
    return x * lax.rsqrt(jnp.mean(x * x, axis=-1, keepdims=True) + EPS) * w


def _silu(x):
    return x * jax.nn.sigmoid(x)


def _split3(x):
    hi = x.astype(BF16)
    r = x - hi.astype(F32)
    mid = r.astype(BF16)
    lo = (r - mid.astype(F32)).astype(BF16)
    return hi, mid, lo


def _in_proj_kernel(x_ref, nw_ref, w_ref, qn_ref, kvn_ref, dtb_ref, ck_ref, sk_ref,
                    z_ref, xbc_ref, cqn_ref, ckv_ref, kr_ref, keys_ref, dt_ref):
    h = _rms(x_ref[...], nw_ref[...]).astype(BF16)
    z_ref[...] = _dot(h, w_ref[:, 0:512])
    xbc_ref[...] = _dot(h, w_ref[:, 512:1536])
    cqn_ref[...] = _rms(_dot(h, w_ref[:, 1536:1792]), qn_ref[...]).astype(BF16)
    tail = _dot(h, w_ref[:, 1792:2048])
    ckv = _rms(tail[:, :LANES], kvn_ref[...])
    ckv_ref[...] = ckv
    misc = tail[:, LANES:]
    kr = (pltpu.roll(misc, LANES - MISC_KR, 1) * ck_ref[...]
          + pltpu.roll(misc, LANES - MISC_KR_SWAPPED, 1) * sk_ref[...])
    kr_ref[...] = kr[:, :QK_ROPE]
    keys_ref[...] = jnp.concatenate([ckv, kr], axis=1).astype(BF16)
    lane = lax.broadcasted_iota(jnp.int32, misc.shape, 1)
    dt_ref[...] = jnp.where(lane < SSD_HEADS, jax.nn.softplus(misc + dtb_ref[...]), 0.0)


def _in_proj(x2d, T, lp, tabs):
    n = x2d.shape[0]
    tm = min(512, n)
    assert n % tm == 0
    if tm <= T:
        assert T % tm == 0
        per_seq = T // tm
        ck, sk = tabs["ck"], tabs["sk"]
        tab_map = lambda i: (i % per_seq, 0)
    else:
        assert tm % T == 0
        ck = jnp.tile(tabs["ck"], (tm // T, 1))
        sk = jnp.tile(tabs["sk"], (tm // T, 1))
        tab_map = lambda i: (0, 0)
    row = lambda i: (i, 0)
    const = lambda i: (0, 0)
    return pl.pallas_call(
        _in_proj_kernel,
        grid=(n // tm,),
        in_specs=[
            pl.BlockSpec((tm, D_MODEL), row),
            pl.BlockSpec((1, D_MODEL), const),
            pl.BlockSpec((D_MODEL, PROJ_WIDTH), const),
            pl.BlockSpec((1, Q_LORA), const),
            pl.BlockSpec((1, KV_LORA), const),
            pl.BlockSpec((1, LANES), const),
            pl.BlockSpec((tm, LANES), tab_map),
            pl.BlockSpec((tm, LANES), tab_map),
        ],
        out_specs=[
            pl.BlockSpec((tm, SSD_WIDTH), row),
            pl.BlockSpec((tm, CONV_DIM), row),
            pl.BlockSpec((tm, Q_LORA), row),
            pl.BlockSpec((tm, KV_LORA), row),
            pl.BlockSpec((tm, QK_ROPE), row),
            pl.BlockSpec((tm, KEY_WIDTH), row),
            pl.BlockSpec((tm, LANES), row),
        ],
        out_shape=[
            jax.ShapeDtypeStruct((n, SSD_WIDTH), F32),
            jax.ShapeDtypeStruct((n, CONV_DIM), F32),
            jax.ShapeDtypeStruct((n, Q_LORA), BF16),
            jax.ShapeDtypeStruct((n, KV_LORA), F32),
            jax.ShapeDtypeStruct((n, QK_ROPE), F32),
            jax.ShapeDtypeStruct((n, KEY_WIDTH), BF16),
            jax.ShapeDtypeStruct((n, LANES), F32),
        ],
        compiler_params=pltpu.CompilerParams(
            dimension_semantics=("arbitrary",), vmem_limit_bytes=VMEM_LIMIT),
        name="in_proj",
    )(x2d, lp["norm_mix"], lp["w_in"], lp["q_norm"], lp["kv_norm"], lp["dt_bias"], ck, sk)


def _ssd_kernel(xbc_ref, z_ref, dt_ref, h0_ref, cb0_ref, cw_ref, cbias_ref, alog_ref, dskip_ref, nw_ref,
                y_ref, hout_ref, cout_ref, st_ref, ubuf_ref, act_ref, *, ts):
    j = pl.program_id(1)
    pad = 8

    @pl.when(j == 0)
    def _():
        st_ref[...] = h0_ref[...]
        ubuf_ref[pad - (CONV_W - 1):pad, :] = cb0_ref[...]

    ubuf_ref[pad:pad + ts, :] = xbc_ref[...]
    yc = cbias_ref[...]
    for k in range(CONV_W):
        off = pad - (CONV_W - 1) + k
        yc = yc + ubuf_ref[off:off + ts, :] * cw_ref[k:k + 1, :]
    act_ref[...] = _silu(yc)
    tail = ubuf_ref[pad + ts - (CONV_W - 1):pad + ts, :]
    ubuf_ref[pad - (CONV_W - 1):pad, :] = tail
    cout_ref[...] = tail

    L = CHUNK
    lane = lax.broadcasted_iota(jnp.int32, (L, LANES), 1)
    sub = lax.broadcasted_iota(jnp.int32, (L, LANES), 0)
    left = lane < L
    tril2 = (lane % L) <= sub
    r64 = lax.broadcasted_iota(jnp.int32, (L, L), 0)
    c64 = lax.broadcasted_iota(jnp.int32, (L, L), 1)
    tril = (c64 <= r64).astype(BF16)
    sel8 = (lax.broadcasted_iota(jnp.int32, (8, LANES), 0)
            == lax.broadcasted_iota(jnp.int32, (8, LANES), 1)).astype(BF16)
    head_lane = lax.broadcasted_iota(jnp.int32, (1, LANES), 1) < SSD_HEADS
    a_neg = jnp.where(head_lane, -jnp.exp(alog_ref[...]), 0.0)

    def pair_rows(v):
        stacked = jnp.concatenate([v, pltpu.roll(v, LANES - 1, 1)], axis=0)
        out = jnp.zeros((8, LANES), F32)
        for part in _split3(stacked):
            out = out + lax.dot_general(sel8, part, _NT, preferred_element_type=F32)
        return out

    def pair_cols(v, k):
        return jnp.where(left, v[:, 2 * k:2 * k + 1], v[:, 2 * k + 1:2 * k + 2])

    def chunk(c, carry):
        r0 = pl.multiple_of(c * L, L)
        dtc = dt_ref[pl.ds(r0, L), :]
        a = dtc * a_neg
        cum = jnp.zeros((L, LANES), F32)
        for part in _split3(a):
            cum = cum + _dot(tril, part)
        cum_rows = pair_rows(cum)
        dt_rows = pair_rows(dtc)
        last = cum[L - 1:L, :]
        exp_last = jnp.exp(last)
        xs = act_ref[pl.ds(r0, L), 0:SSD_WIDTH]
        ys = []
        for g in range(SSD_GROUPS):
            b_g = act_ref[pl.ds(r0, L), SSD_WIDTH + g * D_STATE:SSD_WIDTH + (g + 1) * D_STATE].astype(BF16)
            c_g = act_ref[pl.ds(r0, L),
                          SSD_WIDTH + (SSD_GROUPS + g) * D_STATE:SSD_WIDTH + (SSD_GROUPS + g + 1) * D_STATE].astype(BF16)
            cb2 = lax.dot_general(c_g, jnp.concatenate([b_g, b_g], axis=0), _NT, preferred_element_type=F32)
            for kk in range(SSD_HEADS // SSD_GROUPS // 2):
                k = g * (SSD_HEADS // SSD_GROUPS // 2) + kk
                ccol = pair_cols(cum, k)
                dtcol = pair_cols(dtc, k)
                crow = cum_rows[2 * k:2 * k + 1, :]
                dtrow = dt_rows[2 * k:2 * k + 1, :]
                decay = jnp.exp(jnp.where(tril2, ccol - crow, -jnp.inf))
                w2 = (decay * cb2 * dtrow).astype(BF16)
                xp = xs[:, 2 * k * SSD_HEAD_DIM:(2 * k + 2) * SSD_HEAD_DIM]
                xbd = jnp.concatenate([jnp.where(left, xp, 0.0), jnp.where(left, 0.0, xp)], axis=0).astype(BF16)
                st = st_ref[2 * k * SSD_HEAD_DIM:(2 * k + 2) * SSD_HEAD_DIM, :]
                y = _dot(w2, xbd) + jnp.exp(ccol) * lax.dot_general(
                    c_g, st.astype(BF16), _NT, preferred_element_type=F32)
                ys.append(y)
                last2 = jnp.where(left[0:1, :], last[:, 2 * k:2 * k + 1], last[:, 2 * k + 1:2 * k + 2])
                xw = (xp * (jnp.exp(last2 - ccol) * dtcol)).astype(BF16)
                upd = lax.dot_general(xw, b_g, _TN, preferred_element_type=F32)
                dec2 = jnp.concatenate(
                    [jnp.broadcast_to(exp_last[:, 2 * k:2 * k + 1], (SSD_HEAD_DIM, D_STATE)),
                     jnp.broadcast_to(exp_last[:, 2 * k + 1:2 * k + 2], (SSD_HEAD_DIM, D_STATE))], axis=0)
                st_ref[2 * k * SSD_HEAD_DIM:(2 * k + 2) * SSD_HEAD_DIM, :] = dec2 * st + upd
        y = jnp.concatenate(ys, axis=1) + dskip_ref[...] * xs
        y = y * _silu(z_ref[pl.ds(r0, L), :])
        y_ref[pl.ds(r0, L), :] = _rms(y, nw_ref[...]).astype(BF16)
        return carry

    lax.fori_loop(0, ts // L, chunk, 0)

    @pl.when(j == pl.num_programs(1) - 1)
    def _():
        hout_ref[...] = st_ref[...]


def _ssd(xbc, z, dt, h0, conv0, B, T, lp):
    ts = min(256, T)
    assert T % ts == 0 and ts % CHUNK == 0 and T >= CONV_W - 1
    nt = T // ts
    row = lambda b, j: (b * nt + j, 0)
    const = lambda b, j: (0, 0)
    per_b = lambda b, j: (b, 0, 0)
    state_rows = SSD_HEADS * SSD_HEAD_DIM
    return pl.pallas_call(
        functools.partial(_ssd_kernel, ts=ts),
        grid=(B, nt),
        in_specs=[
            pl.BlockSpec((ts, CONV_DIM), row),
            pl.BlockSpec((ts, SSD_WIDTH), row),
            pl.BlockSpec((ts, LANES), row),
            pl.BlockSpec((None, state_rows, D_STATE), per_b),
            pl.BlockSpec((None, CONV_W - 1, CONV_DIM), per_b),
            pl.BlockSpec((CONV_W, CONV_DIM), const),
            pl.BlockSpec((1, CONV_DIM), const),
            pl.BlockSpec((1, LANES), const),
            pl.BlockSpec((1, SSD_WIDTH), const),
            pl.BlockSpec((1, SSD_WIDTH), const),
        ],
        out_specs=[
            pl.BlockSpec((ts, SSD_WIDTH), row),
            pl.BlockSpec((None, state_rows, D_STATE), per_b),
            pl.BlockSpec((None, CONV_W - 1, CONV_DIM), per_b),
        ],
        out_shape=[
            jax.ShapeDtypeStruct((B * T, SSD_WIDTH), BF16),
            jax.ShapeDtypeStruct((B, state_rows, D_STATE), F32),
            jax.ShapeDtypeStruct((B, CONV_W - 1, CONV_DIM), F32),
        ],
        scratch_shapes=[
            pltpu.VMEM((state_rows, D_STATE), F32),
            pltpu.VMEM((ts + 8, CONV_DIM), F32),
            pltpu.VMEM((ts, CONV_DIM), F32),
        ],
        compiler_params=pltpu.CompilerParams(
            dimension_semantics=("arbitrary", "arbitrary"), vmem_limit_bytes=VMEM_LIMIT),
        name="ssd",
    )(xbc, z, dt, h0, conv0, lp["conv_w"], lp["conv_b"], lp["a_log"], lp["d_skip"], lp["ssd_norm"])


def _attn_kernel(cqn_ref, keys_ref, wq_ref, wuk_ref, wuv_ref, cq_ref, sq_ref, o_ref,
                 q_scr, v_scr, s_scr, m_scr, acc_scr, *, tq, tk, past, n_masked):
    i = pl.program_id(1)
    rows = MLA_HEADS * tq
    lane_tiles = tk // LANES

    @pl.when(i == 0)
    def _():
        kv = keys_ref[...]
        lane = lax.broadcasted_iota(jnp.int32, kv.shape, 1)
        v_scr[...] = jnp.where(lane < KV_LORA, kv, jnp.where(lane == KV_LORA, 1.0, 0.0).astype(BF16))

    qall = _dot(cqn_ref[...], wq_ref[...])
    for h in range(MLA_HEADS):
        t = qall[:, h * LANES:(h + 1) * LANES]
        r = t * cq_ref[...] + pltpu.roll(t, LANES - QK_ROPE, 1) * sq_ref[...]
        q_scr[h * tq:(h + 1) * tq, :] = _dot(r.astype(BF16), wuk_ref[h]).astype(BF16)
    m_scr[...] = jnp.full(m_scr.shape, -jnp.inf, F32)

    q_first = past + i * tq
    n_full = q_first // tk

    def scores(jt, masked):
        k0 = pl.multiple_of(jt * tk, tk)
        s = lax.dot_general(q_scr[...], keys_ref[pl.ds(k0, tk), :], _NT, preferred_element_type=F32)
        if masked:
            qc = (q_first + lax.broadcasted_iota(jnp.int32, (tq, tk), 0)) // CHUNK
            kc = (k0 + lax.broadcasted_iota(jnp.int32, (tq, tk), 1)) // CHUNK
            bias = jnp.where(kc <= qc, 0.0, -jnp.inf).astype(F32)
            s = (s.reshape(MLA_HEADS, tq, tk) + bias[None]).reshape(rows, tk)
        s_scr[jt] = s
        m = m_scr[...]
        for c in range(lane_tiles):
            m = jnp.maximum(m, s[:, c * LANES:(c + 1) * LANES])
        if tk % LANES:
            m = jnp.maximum(m, jnp.max(s[:, lane_tiles * LANES:], axis=-1, keepdims=True))
        m_scr[...] = m

    def full_scores(jt, carry):
        scores(jt, False)
        return carry

    lax.fori_loop(0, n_full, full_scores, 0)
    for d in range(n_masked):
        scores(n_full + d, True)

    m_row = jnp.max(m_scr[...], axis=-1, keepdims=True)
    m_scr[...] = jnp.broadcast_to(m_row, m_scr.shape)
    acc_scr[...] = jnp.zeros(acc_scr.shape, F32)

    def values(jt, carry):
        k0 = pl.multiple_of(jt * tk, tk)
        mb = m_scr[...]
        if tk % LANES:
            mb = mb[:, 0:1]
        else:
            mb = jnp.concatenate([mb] * lane_tiles, axis=1)
        p = jnp.exp2(s_scr[jt] - mb).astype(BF16)
        acc_scr[...] += _dot(p, v_scr[pl.ds(k0, tk), :])
        return carry

    lax.fori_loop(0, n_full + n_masked, values, 0)

    acc = acc_scr[...]
    o_lat = acc[:, :KV_LORA] / acc[:, KV_LORA:KV_LORA + 1]
    for k in range(MLA_HEADS // 2):
        pair = jnp.concatenate([o_lat[(2 * k) * tq:(2 * k + 1) * tq, :],
                                o_lat[(2 * k + 1) * tq:(2 * k + 2) * tq, :]], axis=1).astype(BF16)
        o_ref[:, k * LANES:(k + 1) * LANES] = _dot(pair, wuv_ref[k]).astype(BF16)


def _attn(cqn, keys3d, B, T, past, lp, tabs):
    S = keys3d.shape[1]
    assert S == past + T
    if T >= 256:
        tq = tk = 256
        assert T % tq == 0 and past % tk == 0
        n_masked = tq // tk
    else:
        tq, tk = T, S
        n_masked = 1
    nq = T // tq
    rows = MLA_HEADS * tq
    const2 = lambda b, i: (0, 0)
    const3 = lambda b, i: (0, 0, 0)
    return pl.pallas_call(
        functools.partial(_attn_kernel, tq=tq, tk=tk, past=past, n_masked=n_masked),
        grid=(B, nq),
        in_specs=[
            pl.BlockSpec((tq, Q_LORA), lambda b, i: (b * nq + i, 0)),
            pl.BlockSpec((None, S, KEY_WIDTH), lambda b, i: (b, 0, 0)),
            pl.BlockSpec((Q_LORA, MLA_HEADS * LANES), const2),
            pl.BlockSpec((MLA_HEADS, LANES, KEY_WIDTH), const3),
            pl.BlockSpec((MLA_HEADS // 2, 2 * KV_LORA, LANES), const3),
            pl.BlockSpec((tq, LANES), lambda b, i: (i, 0)),
            pl.BlockSpec((tq, LANES), lambda b, i: (i, 0)),
        ],
        out_specs=pl.BlockSpec((tq, MLA_WIDTH), lambda b, i: (b * nq + i, 0)),
        out_shape=jax.ShapeDtypeStruct((B * T, MLA_WIDTH), BF16),
        scratch_shapes=[
            pltpu.VMEM((rows, KEY_WIDTH), BF16),
            pltpu.VMEM((S, KEY_WIDTH), BF16),
            pltpu.VMEM((S // tk, rows, tk), F32),
            pltpu.VMEM((rows, LANES), F32),
            pltpu.VMEM((rows, KEY_WIDTH), F32),
        ],
        compiler_params=pltpu.CompilerParams(
            dimension_semantics=("arbitrary", "arbitrary"), vmem_limit_bytes=VMEM_LIMIT),
        name="attn",
    )(cqn, keys3d, lp["w_q"], lp["w_uk"], lp["w_uv"], tabs["cq"], tabs["sq"])


def _ffn_kernel(x_ref, y_ref, o_ref, woa_ref, wob_ref, nf_ref, wg_ref, wu_ref, wd_ref, nfin_ref,
                xo_ref, *, final):
    x1 = x_ref[...] + _dot(y_ref[...], woa_ref[...]) + _dot(o_ref[...], wob_ref[...])
    hf = _rms(x1, nf_ref[...]).astype(BF16)
    g = _dot(hf, wg_ref[...])
    u = _dot(hf, wu_ref[...])
    x2 = x1 + _dot((_silu(g) * u).astype(BF16), wd_ref[...])
    if final:
        x2 = _rms(x2, nfin_ref[...])
    xo_ref[...] = x2


def _ffn(x2d, y, o, lp, norm_final, final):
    n = x2d.shape[0]
    tm = min(256, n)
    assert n % tm == 0
    row = lambda i: (i, 0)
    const = lambda i: (0, 0)
    resident = functools.partial(pl.BlockSpec, index_map=const)
    return pl.pallas_call(
        functools.partial(_ffn_kernel, final=final),
        grid=(n // tm,),
        in_specs=[
            pl.BlockSpec((tm, D_MODEL), row),
            pl.BlockSpec((tm, SSD_WIDTH), row),
            pl.BlockSpec((tm, MLA_WIDTH), row),
            resident((SSD_WIDTH, D_MODEL)),
            resident((MLA_WIDTH, D_MODEL)),
            pl.BlockSpec((1, D_MODEL), const),
            resident((D_MODEL, D_FF)),
            resident((D_MODEL, D_FF)),
            resident((D_FF, D_MODEL)),
            pl.BlockSpec((1, D_MODEL), const),
        ],
        out_specs=pl.BlockSpec((tm, D_MODEL), row),
        out_shape=jax.ShapeDtypeStruct((n, D_MODEL), F32),
        compiler_params=pltpu.CompilerParams(
            dimension_semantics=("arbitrary",), vmem_limit_bytes=VMEM_LIMIT),
        name="ffn",
    )(x2d, y, o, lp["w_out_a"], lp["w_out_b"], lp["norm_ffn"], lp["w_gate"], lp["w_up"], lp["w_down"], norm_final)


def _swap_halves_cols(w):
    half = w.shape[-1] // 2
    return jnp.concatenate([w[..., half:], w[..., :half]], axis=-1)


def _layer_params(l, w_in, w_uq, w_uk, w_uv, w_out, norm_mix, q_norm, kv_norm, ssd_norm,
                  conv_w, conv_b, dt_bias, a_log, d_skip, norm_ffn, w_gate, w_up, w_down):
    wi = w_in[l]
    w_kr = wi[:, S_Q + KV_LORA:]
    misc = jnp.concatenate([
        wi[:, S_XBC:S_DT], jnp.zeros((D_MODEL, MISC_KR - SSD_HEADS), F32),
        w_kr, _swap_halves_cols(w_kr),
        jnp.zeros((D_MODEL, LANES - MISC_KR_SWAPPED - QK_ROPE), F32)], axis=1)
    w_in_r = jnp.concatenate([wi[:, :S_XBC], wi[:, S_DT:S_Q], wi[:, S_Q:S_Q + KV_LORA], misc], axis=1)

    wq = w_uq[l].reshape(Q_LORA, MLA_HEADS, QK_NOPE + QK_ROPE)
    wq_rope = wq[:, :, QK_NOPE:]
    w_q = jnp.concatenate([wq, _swap_halves_cols(wq_rope)], axis=-1).reshape(Q_LORA, MLA_HEADS * LANES)

    uk_t = jnp.transpose(w_uk[l], (1, 2, 0))
    eye = jnp.eye(QK_ROPE, dtype=F32)
    top = jnp.concatenate([uk_t, jnp.zeros((MLA_HEADS, QK_NOPE, KEY_WIDTH - KV_LORA), F32)], axis=2)
    mid = jnp.concatenate([jnp.zeros((QK_ROPE, KV_LORA), F32), eye,
                           jnp.zeros((QK_ROPE, KEY_WIDTH - KV_LORA - QK_ROPE), F32)], axis=1)
    mid = jnp.broadcast_to(mid[None], (MLA_HEADS, QK_ROPE, KEY_WIDTH))
    bot = jnp.zeros((MLA_HEADS, LANES - QK_NOPE - QK_ROPE, KEY_WIDTH), F32)
    w_uk_p = jnp.concatenate([top, mid, bot], axis=1)

    uv = jnp.transpose(w_uv[l], (1, 0, 2)).reshape(MLA_HEADS // 2, 2, KV_LORA, V_HEAD)
    zero = jnp.zeros((MLA_HEADS // 2, KV_LORA, V_HEAD), F32)
    w_uv_p = jnp.concatenate([jnp.concatenate([uv[:, 0], zero], axis=2),
                              jnp.concatenate([zero, uv[:, 1]], axis=2)], axis=1)

    pad_heads = lambda v: jnp.concatenate([v, jnp.zeros((LANES - SSD_HEADS,), F32)])[None]
    return {
        "w_in": w_in_r.astype(BF16), "w_q": w_q.astype(BF16), "w_uk": w_uk_p.astype(BF16),
        "w_uv": w_uv_p.astype(BF16),
        "w_out_a": w_out[l][:SSD_WIDTH].astype(BF16), "w_out_b": w_out[l][SSD_WIDTH:].astype(BF16),
        "w_gate": w_gate[l].astype(BF16), "w_up": w_up[l].astype(BF16), "w_down": w_down[l].astype(BF16),
        "norm_mix": norm_mix[l][None], "q_norm": q_norm[l][None], "kv_norm": kv_norm[l][None],
        "ssd_norm": ssd_norm[l][None], "norm_ffn": norm_ffn[l][None],
        "conv_w": conv_w[l], "conv_b": conv_b[l][None],
        "dt_bias": pad_heads(dt_bias[l]), "a_log": pad_heads(a_log[l]),
        "d_skip": jnp.repeat(d_skip[l], SSD_HEAD_DIM)[None],
    }


def _rope_tables(past, T):
    pos = (past + jnp.arange(T)).astype(F32)
    inv = 1.0 / (ROPE_BASE ** (jnp.arange(0, QK_ROPE, 2, dtype=F32) / QK_ROPE))
    ang = pos[:, None] * inv[None, :]
    cos, sin = jnp.cos(ang), jnp.sin(ang)
    c32 = jnp.concatenate([cos, cos], axis=1)
    s32 = jnp.concatenate([-sin, sin], axis=1)
    scale = (QK_NOPE + QK_ROPE) ** -0.5 * LOG2_E
    zeros = lambda w: jnp.zeros((T, w), F32)
    return {
        "ck": jnp.concatenate([c32, zeros(LANES - QK_ROPE)], axis=1),
        "sk": jnp.concatenate([s32, zeros(LANES - QK_ROPE)], axis=1),
        "cq": scale * jnp.concatenate([jnp.ones((T, QK_NOPE), F32), c32, zeros(LANES - QK_NOPE - QK_ROPE)], axis=1),
        "sq": scale * jnp.concatenate([zeros(QK_NOPE), s32, zeros(LANES - QK_NOPE - QK_ROPE)], axis=1),
    }


def _hybrid_layer(x2d, B, T, past_keys, h0, conv0, lp, tabs, norm_final, final):
    past = 0 if past_keys is None else past_keys.shape[1]
    z, xbc, cqn, ckv, kr, keys, dt = _in_proj(x2d, T, lp, tabs)
    y, h_new, conv_new = _ssd(xbc, z, dt, h0, conv0, B, T, lp)
    keys3d = keys.reshape(B, T, KEY_WIDTH)
    if past_keys is not None:
        keys3d = jnp.concatenate([past_keys, keys3d], axis=1)
    o = _attn(cqn, keys3d, B, T, past, lp, tabs)
    x_new = _ffn(x2d, y, o, lp, norm_final, final)
    return x_new, ckv, kr, h_new, conv_new


def kernel(x_prompt, x_sample, cache_mla_ckv, cache_mla_krope, state_ssm, state_conv, w_in, w_uq, w_uk, w_uv, w_out, norm_mix, q_norm, kv_norm, ssd_norm, conv_w, conv_b, dt_bias, a_log, d_skip, norm_ffn, w_gate, w_up, w_down, norm_final):
    depth = w_in.shape[0]
    bp, tp, _ = x_prompt.shape
    bs, tsm, _ = x_sample.shape
    past = cache_mla_ckv.shape[2]
    state_rows = SSD_HEADS * SSD_HEAD_DIM

    xp = x_prompt.reshape(bp * tp, D_MODEL)
    xs = x_sample.reshape(bs * tsm, D_MODEL)
    tabs_p = _rope_tables(0, tp)
    tabs_s = _rope_tables(past, tsm)
    zero_state = jnp.zeros((bp, state_rows, D_STATE), F32)
    zero_conv = jnp.zeros((bp, CONV_W - 1, CONV_DIM), F32)
    nfin = norm_final[None]

    outs_p = [[], [], [], []]
    outs_s = [[], [], [], []]
    for l in range(depth):
        lp = _layer_params(l, w_in, w_uq, w_uk, w_uv, w_out, norm_mix, q_norm, kv_norm, ssd_norm,
                           conv_w, conv_b, dt_bias, a_log, d_skip, norm_ffn, w_gate, w_up, w_down)
        final = l == depth - 1
        xp, ckv, kr, hs, cb = _hybrid_layer(xp, bp, tp, None, zero_state, zero_conv, lp, tabs_p, nfin, final)
        for acc, v in zip(outs_p, (ckv.reshape(bp, tp, KV_LORA), kr.reshape(bp, tp, QK_ROPE),
                                   hs.reshape(bp, SSD_HEADS, SSD_HEAD_DIM, D_STATE), cb)):
            acc.append(v)
        past_keys = jnp.concatenate(
            [cache_mla_ckv[l], cache_mla_krope[l],
             jnp.zeros((bs, past, KEY_WIDTH - KV_LORA - QK_ROPE), F32)], axis=-1).astype(BF16)
        xs, ckv, kr, hs, cb = _hybrid_layer(
            xs, bs, tsm, past_keys, state_ssm[l].reshape(bs, state_rows, D_STATE), state_conv[l],
            lp, tabs_s, nfin, final)
        for acc, v in zip(outs_s, (ckv.reshape(bs, tsm, KV_LORA), kr.reshape(bs, tsm, QK_ROPE),
                                   hs.reshape(bs, SSD_HEADS, SSD_HEAD_DIM, D_STATE), cb)):
            acc.append(v)
    return (xp.reshape(bp, tp, D_MODEL), xs.reshape(bs, tsm, D_MODEL),
            *(jnp.stack(v) for v in outs_p), *(jnp.stack(v) for v in outs_s))
```

```python
import functools

import jax
import jax.numpy as jnp
from jax import lax
from jax.experimental import pallas as pl
from jax.experimental.pallas import tpu as pltpu

F32 = jnp.float32
BF16 = jnp.bfloat16

D_MODEL = 1024
CHUNK = 64
EPS = 1e-6
SSD_HEADS = 8
SSD_HEAD_DIM = 64
SSD_WIDTH = SSD_HEADS * SSD_HEAD_DIM
SSD_GROUPS = 2
D_STATE = 128
CONV_W = 4
CONV_DIM = SSD_WIDTH + 2 * SSD_GROUPS * D_STATE
MLA_HEADS = 8
QK_NOPE = 64
QK_ROPE = 32
V_HEAD = 64
Q_LORA = 256
KV_LORA = 128
MLA_WIDTH = MLA_HEADS * V_HEAD
ROPE_BASE = 10000.0
LOG2_E = 1.4426950408889634
S_Z = SSD_WIDTH
S_XBC = S_Z + CONV_DIM
S_DT = S_XBC + SSD_HEADS
S_Q = S_DT + Q_LORA
D_FF = 2816

LANES = 128
SUBLANES = 8
KEY_WIDTH = 2 * LANES
PROJ_WIDTH = 2048
MISC_KR = 32
MISC_KR_SWAPPED = 64
VMEM_LIMIT = 56 * 1024 * 1024
IN_PROJ_SUBTILE = 256
FFN_SUBTILE = 256

_NT = (((1,), (1,)), ((), ()))
_TN = (((0,), (0,)), ((), ()))


def _dot(a, b):
    return jnp.dot(a, b, preferred_element_type=F32)


def _rms(x, w):
    return x * lax.rsqrt(jnp.mean(x * x, axis=-1, keepdims=True) + EPS) * w


def _silu(x):
    return x * jax.nn.sigmoid(x)


def _split3(x):
    hi = x.astype(BF16)
    r = x - hi.astype(F32)
    mid = r.astype(BF16)
    lo = (r - mid.astype(F32)).astype(BF16)
    return hi, mid, lo


def _in_proj_kernel(x_ref, nw_ref, w_ref, qn_ref, kvn_ref, dtb_ref, ck_ref, sk_ref,
                    z_ref, xbc_ref, cqn_ref, ckv_ref, kr_ref, keys_ref, dt_ref):
    tm = x_ref.shape[0]
    sub = min(tm, IN_PROJ_SUBTILE)
    for r0 in range(0, tm, sub):
        rows = slice(r0, r0 + sub)
        h = _rms(x_ref[rows, :], nw_ref[...]).astype(BF16)
        z_ref[rows, :] = _dot(h, w_ref[:, 0:512])
        xbc_ref[rows, :] = _dot(h, w_ref[:, 512:1536])
        cqn_ref[rows, :] = _rms(_dot(h, w_ref[:, 1536:1792]), qn_ref[...]).astype(BF16)
        tail = _dot(h, w_ref[:, 1792:2048])
        ckv = _rms(tail[:, :LANES], kvn_ref[...])
        ckv_ref[rows, :] = ckv
        misc = tail[:, LANES:]
        kr = (pltpu.roll(misc, LANES - MISC_KR, 1) * ck_ref[rows, :]
              + pltpu.roll(misc, LANES - MISC_KR_SWAPPED, 1) * sk_ref[rows, :])
        kr_ref[rows, :] = kr[:, :QK_ROPE]
        keys_ref[rows, :] = jnp.concatenate([ckv, kr], axis=1).astype(BF16)
        lane = lax.broadcasted_iota(jnp.int32, misc.shape, 1)
        dt_ref[rows, :] = jnp.where(lane < SSD_HEADS, jax.nn.softplus(misc + dtb_ref[...]), 0.0)


def _in_proj(x2d, T, lp, tabs):
    n = x2d.shape[0]
    tm = min(512, n)
    assert n % tm == 0
    if tm <= T:
        assert T % tm == 0
        per_seq = T // tm
        ck, sk = tabs["ck"], tabs["sk"]
        tab_map = lambda i: (i % per_seq, 0)
    else:
        assert tm % T == 0
        ck = jnp.tile(tabs["ck"], (tm // T, 1))
        sk = jnp.tile(tabs["sk"], (tm // T, 1))
        tab_map = lambda i: (0, 0)
    row = lambda i: (i, 0)
    const = lambda i: (0, 0)
    return pl.pallas_call(
        _in_proj_kernel,
        grid=(n // tm,),
        in_specs=[
            pl.BlockSpec((tm, D_MODEL), row),
            pl.BlockSpec((1, D_MODEL), const),
            pl.BlockSpec((D_MODEL, PROJ_WIDTH), const),
            pl.BlockSpec((1, Q_LORA), const),
            pl.BlockSpec((1, KV_LORA), const),
            pl.BlockSpec((1, LANES), const),
            pl.BlockSpec((tm, LANES), tab_map),
            pl.BlockSpec((tm, LANES), tab_map),
        ],
        out_specs=[
            pl.BlockSpec((tm, SSD_WIDTH), row),
            pl.BlockSpec((tm, CONV_DIM), row),
            pl.BlockSpec((tm, Q_LORA), row),
            pl.BlockSpec((tm, KV_LORA), row),
            pl.BlockSpec((tm, QK_ROPE), row),
            pl.BlockSpec((tm, KEY_WIDTH), row),
            pl.BlockSpec((tm, LANES), row),
        ],
        out_shape=[
            jax.ShapeDtypeStruct((n, SSD_WIDTH), F32),
            jax.ShapeDtypeStruct((n, CONV_DIM), F32),
            jax.ShapeDtypeStruct((n, Q_LORA), BF16),
            jax.ShapeDtypeStruct((n, KV_LORA), F32),
            jax.ShapeDtypeStruct((n, QK_ROPE), F32),
            jax.ShapeDtypeStruct((n, KEY_WIDTH), BF16),
            jax.ShapeDtypeStruct((n, LANES), F32),
        ],
        compiler_params=pltpu.CompilerParams(
            dimension_semantics=("arbitrary",), vmem_limit_bytes=VMEM_LIMIT),
        name="in_proj",
    )(x2d, lp["norm_mix"], lp["w_in"], lp["q_norm"], lp["kv_norm"], lp["dt_bias"], ck, sk)


def _ssd_kernel(xbc_ref, z_ref, dt_ref, h0_ref, cb0_ref, cw_ref, cbias_ref, alog_ref, dskip_ref, nw_ref,
                y_ref, hout_ref, cout_ref, st_ref, ubuf_ref, act_ref, *, ts):
    j = pl.program_id(1)
    pad = SUBLANES

    @pl.when(j == 0)
    def _():
        st_ref[...] = h0_ref[...]
        ubuf_ref[0:pad, :] = jnp.zeros((pad, CONV_DIM), F32)
        ubuf_ref[pad - (CONV_W - 1):pad, :] = cb0_ref[...]

    ubuf_ref[pad:pad + ts, :] = xbc_ref[...]
    cur = ubuf_ref[pad:pad + ts, :].reshape(ts // SUBLANES, SUBLANES, CONV_DIM)
    above = ubuf_ref[0:ts, :].reshape(ts // SUBLANES, SUBLANES, CONV_DIM)
    sub3 = lax.broadcasted_iota(jnp.int32, cur.shape, 1)
    yc = cbias_ref[...].reshape(1, 1, CONV_DIM)
    for k in range(CONV_W):
        d = CONV_W - 1 - k
        u = cur if d == 0 else pltpu.roll(jnp.where(sub3 >= SUBLANES - d, above, cur), d, 1)
        yc = yc + u * cw_ref[k:k + 1, :].reshape(1, 1, CONV_DIM)
    act_ref[...] = _silu(yc).reshape(ts, CONV_DIM)
    cout_ref[...] = ubuf_ref[pad + ts - (CONV_W - 1):pad + ts, :]
    ubuf_ref[0:pad, :] = ubuf_ref[ts:ts + pad, :]

    L = CHUNK
    lane = lax.broadcasted_iota(jnp.int32, (L, LANES), 1)
    sub = lax.broadcasted_iota(jnp.int32, (L, LANES), 0)
    left = lane < L
    tril2 = (lane % L) <= sub
    r64 = lax.broadcasted_iota(jnp.int32, (L, L), 0)
    c64 = lax.broadcasted_iota(jnp.int32, (L, L), 1)
    tril = (c64 <= r64).astype(BF16)
    sel8 = (lax.broadcasted_iota(jnp.int32, (8, LANES), 0)
            == lax.broadcasted_iota(jnp.int32, (8, LANES), 1)).astype(BF16)
    head_lane = lax.broadcasted_iota(jnp.int32, (1, LANES), 1) < SSD_HEADS
    a_neg = jnp.where(head_lane, -jnp.exp(alog_ref[...]), 0.0)

    def pair_rows(v):
        stacked = jnp.concatenate([v, pltpu.roll(v, LANES - 1, 1)], axis=0)
        out = jnp.zeros((8, LANES), F32)
        for part in _split3(stacked):
            out = out + lax.dot_general(sel8, part, _NT, preferred_element_type=F32)
        return out

    def pair_cols(v, k):
        return jnp.where(left, v[:, 2 * k:2 * k + 1], v[:, 2 * k + 1:2 * k + 2])

    def chunk(c):
        r0 = c * L
        dtc = dt_ref[pl.ds(r0, L), :]
        a = dtc * a_neg
        cum = jnp.zeros((L, LANES), F32)
        for part in _split3(a):
            cum = cum + _dot(tril, part)
        cum_rows = pair_rows(cum)
        dt_rows = pair_rows(dtc)
        last = cum[L - 1:L, :]
        exp_last = jnp.exp(last)
        xs = act_ref[pl.ds(r0, L), 0:SSD_WIDTH]
        ys = []
        for g in range(SSD_GROUPS):
            b_g = act_ref[pl.ds(r0, L), SSD_WIDTH + g * D_STATE:SSD_WIDTH + (g + 1) * D_STATE].astype(BF16)
            c_g = act_ref[pl.ds(r0, L),
                          SSD_WIDTH + (SSD_GROUPS + g) * D_STATE:SSD_WIDTH + (SSD_GROUPS + g + 1) * D_STATE].astype(BF16)
            cb2 = lax.dot_general(c_g, jnp.concatenate([b_g, b_g], axis=0), _NT, preferred_element_type=F32)
            for kk in range(SSD_HEADS // SSD_GROUPS // 2):
                k = g * (SSD_HEADS // SSD_GROUPS // 2) + kk
                ccol = pair_cols(cum, k)
                dtcol = pair_cols(dtc, k)
                crow = cum_rows[2 * k:2 * k + 1, :]
                dtrow = dt_rows[2 * k:2 * k + 1, :]
                decay = jnp.exp(jnp.where(tril2, ccol - crow, -jnp.inf))
                w2 = (decay * cb2 * dtrow).astype(BF16)
                xp = xs[:, 2 * k * SSD_HEAD_DIM:(2 * k + 2) * SSD_HEAD_DIM]
                xbd = jnp.concatenate([jnp.where(left, xp, 0.0), jnp.where(left, 0.0, xp)], axis=0).astype(BF16)
                st = st_ref[2 * k * SSD_HEAD_DIM:(2 * k + 2) * SSD_HEAD_DIM, :]
                y = _dot(w2, xbd) + jnp.exp(ccol) * lax.dot_general(
                    c_g, st.astype(BF16), _NT, preferred_element_type=F32)
                ys.append(y)
                last2 = jnp.where(left[0:1, :], last[:, 2 * k:2 * k + 1], last[:, 2 * k + 1:2 * k + 2])
                xw = (xp * (jnp.exp(last2 - ccol) * dtcol)).astype(BF16)
                upd = lax.dot_general(xw, b_g, _TN, preferred_element_type=F32)
                dec2 = jnp.concatenate(
                    [jnp.broadcast_to(exp_last[:, 2 * k:2 * k + 1], (SSD_HEAD_DIM, D_STATE)),
                     jnp.broadcast_to(exp_last[:, 2 * k + 1:2 * k + 2], (SSD_HEAD_DIM, D_STATE))], axis=0)
                st_ref[2 * k * SSD_HEAD_DIM:(2 * k + 2) * SSD_HEAD_DIM, :] = dec2 * st + upd
        y = jnp.concatenate(ys, axis=1) + dskip_ref[...] * xs
        y = y * _silu(z_ref[pl.ds(r0, L), :])
        y_ref[pl.ds(r0, L), :] = _rms(y, nw_ref[...]).astype(BF16)

    for c in range(ts // L):
        chunk(c)

    @pl.when(j == pl.num_programs(1) - 1)
    def _():
        hout_ref[...] = st_ref[...]


def _ssd(xbc, z, dt, h0, conv0, B, T, lp):
    ts = min(256, T)
    assert T % ts == 0 and ts % CHUNK == 0 and T >= CONV_W - 1
    nt = T // ts
    row = lambda b, j: (b * nt + j, 0)
    const = lambda b, j: (0, 0)
    per_b = lambda b, j: (b, 0, 0)
    state_rows = SSD_HEADS * SSD_HEAD_DIM
    return pl.pallas_call(
        functools.partial(_ssd_kernel, ts=ts),
        grid=(B, nt),
        in_specs=[
            pl.BlockSpec((ts, CONV_DIM), row),
            pl.BlockSpec((ts, SSD_WIDTH), row),
            pl.BlockSpec((ts, LANES), row),
            pl.BlockSpec((None, state_rows, D_STATE), per_b),
            pl.BlockSpec((None, CONV_W - 1, CONV_DIM), per_b),
            pl.BlockSpec((CONV_W, CONV_DIM), const),
            pl.BlockSpec((1, CONV_DIM), const),
            pl.BlockSpec((1, LANES), const),
            pl.BlockSpec((1, SSD_WIDTH), const),
            pl.BlockSpec((1, SSD_WIDTH), const),
        ],
        out_specs=[
            pl.BlockSpec((ts, SSD_WIDTH), row),
            pl.BlockSpec((None, state_rows, D_STATE), per_b),
            pl.BlockSpec((None, CONV_W - 1, CONV_DIM), per_b),
        ],
        out_shape=[
            jax.ShapeDtypeStruct((B * T, SSD_WIDTH), BF16),
            jax.ShapeDtypeStruct((B, state_rows, D_STATE), F32),
            jax.ShapeDtypeStruct((B, CONV_W - 1, CONV_DIM), F32),
        ],
        scratch_shapes=[
            pltpu.VMEM((state_rows, D_STATE), F32),
            pltpu.VMEM((ts + 8, CONV_DIM), F32),
            pltpu.VMEM((ts, CONV_DIM), F32),
        ],
        compiler_params=pltpu.CompilerParams(
            dimension_semantics=("arbitrary", "arbitrary"), vmem_limit_bytes=VMEM_LIMIT),
        name="ssd",
    )(xbc, z, dt, h0, conv0, lp["conv_w"], lp["conv_b"], lp["a_log"], lp["d_skip"], lp["ssd_norm"])


def _attn_kernel(cqn_ref, keys_ref, wq_ref, wuk_ref, wuv_ref, cq_ref, sq_ref, o_ref,
                 q_scr, v_scr, s_scr, m_scr, acc_scr, *, tq, tk, past, n_masked):
    i = pl.program_id(1)
    rows = MLA_HEADS * tq
    lane_tiles = tk // LANES

    @pl.when(i == 0)
    def _():
        kv = keys_ref[...]
        lane = lax.broadcasted_iota(jnp.int32, kv.shape, 1)
        v_scr[...] = jnp.where(lane < KV_LORA, kv, jnp.where(lane == KV_LORA, 1.0, 0.0).astype(BF16))

    qall = _dot(cqn_ref[...], wq_ref[...])
    for h in range(MLA_HEADS):
        t = qall[:, h * LANES:(h + 1) * LANES]
        r = t * cq_ref[...] + pltpu.roll(t, LANES - QK_ROPE, 1) * sq_ref[...]
        q_scr[h * tq:(h + 1) * tq, :] = _dot(r.astype(BF16), wuk_ref[h]).astype(BF16)
    m_scr[...] = jnp.full(m_scr.shape, -jnp.inf, F32)

    q_first = past + i * tq
    n_full = q_first // tk

    def for_each_run(first, count, fn):
        quads = count // 4

        def quad(t, carry):
            fn(first + 4 * t, 4)
            return carry

        lax.fori_loop(0, quads, quad, 0)
        start = first + 4 * quads
        for width in (2, 1):
            take = (count & width) != 0

            @pl.when(take)
            def _():
                fn(start, width)

            start = start + jnp.where(take, width, 0)

    def scores(jt, width, masked=False):
        k0 = pl.multiple_of(jt * tk, tk)
        s = lax.dot_general(q_scr[...], keys_ref[pl.ds(k0, width * tk), :], _NT,
                            preferred_element_type=F32)
        if masked:
            qc = (q_first + lax.broadcasted_iota(jnp.int32, (tq, width * tk), 0)) // CHUNK
            kc = (k0 + lax.broadcasted_iota(jnp.int32, (tq, width * tk), 1)) // CHUNK
            bias = jnp.where(kc <= qc, 0.0, -jnp.inf).astype(F32)
            s = (s.reshape(MLA_HEADS, tq, width * tk) + bias[None]).reshape(rows, width * tk)
        m = m_scr[...]
        for w in range(width):
            sw = s[:, w * tk:(w + 1) * tk]
            s_scr[jt + w] = sw
            for c in range(lane_tiles):
                m = jnp.maximum(m, sw[:, c * LANES:(c + 1) * LANES])
            if tk % LANES:
                m = jnp.maximum(m, jnp.max(sw[:, lane_tiles * LANES:], axis=-1, keepdims=True))
        m_scr[...] = m

    for_each_run(0, n_full, scores)
    scores(n_full, n_masked, masked=True)

    m_row = jnp.max(m_scr[...], axis=-1, keepdims=True)
    m_scr[...] = jnp.broadcast_to(m_row, m_scr.shape)
    acc_scr[...] = jnp.zeros(acc_scr.shape, F32)

    def values(jt, width):
        k0 = pl.multiple_of(jt * tk, tk)
        mb = m_scr[...]
        if tk % LANES:
            mb = mb[:, 0:1]
        else:
            mb = jnp.concatenate([mb] * lane_tiles, axis=1)
        p = jnp.concatenate([jnp.exp2(s_scr[jt + w] - mb).astype(BF16) for w in range(width)], axis=1)
        acc_scr[...] += _dot(p, v_scr[pl.ds(k0, width * tk), :])

    for_each_run(0, n_full + n_masked, values)

    acc = acc_scr[...]
    o_lat = acc[:, :KV_LORA] / acc[:, KV_LORA:KV_LORA + 1]
    for k in range(MLA_HEADS // 2):
        pair = jnp.concatenate([o_lat[(2 * k) * tq:(2 * k + 1) * tq, :],
                                o_lat[(2 * k + 1) * tq:(2 * k + 2) * tq, :]], axis=1).astype(BF16)
        o_ref[:, k * LANES:(k + 1) * LANES] = _dot(pair, wuv_ref[k]).astype(BF16)


def _attn(cqn, keys3d, B, T, past, lp, tabs):
    S = keys3d.shape[1]
    assert S == past + T
    if T >= 256:
        tq = tk = 256
        assert T % tq == 0 and past % tk == 0
        n_masked = tq // tk
    else:
        tq, tk = T, S
        n_masked = 1
    nq = T // tq
    rows = MLA_HEADS * tq
    const2 = lambda b, i: (0, 0)
    const3 = lambda b, i: (0, 0, 0)
    return pl.pallas_call(
        functools.partial(_attn_kernel, tq=tq, tk=tk, past=past, n_masked=n_masked),
        grid=(B, nq),
        in_specs=[
            pl.BlockSpec((tq, Q_LORA), lambda b, i: (b * nq + i, 0)),
            pl.BlockSpec((None, S, KEY_WIDTH), lambda b, i: (b, 0, 0)),
            pl.BlockSpec((Q_LORA, MLA_HEADS * LANES), const2),
            pl.BlockSpec((MLA_HEADS, LANES, KEY_WIDTH), const3),
            pl.BlockSpec((MLA_HEADS // 2, 2 * KV_LORA, LANES), const3),
            pl.BlockSpec((tq, LANES), lambda b, i: (i, 0)),
            pl.BlockSpec((tq, LANES), lambda b, i: (i, 0)),
        ],
        out_specs=pl.BlockSpec((tq, MLA_WIDTH), lambda b, i: (b * nq + i, 0)),
        out_shape=jax.ShapeDtypeStruct((B * T, MLA_WIDTH), BF16),
        scratch_shapes=[
            pltpu.VMEM((rows, KEY_WIDTH), BF16),
            pltpu.VMEM((S, KEY_WIDTH), BF16),
            pltpu.VMEM((S // tk, rows, tk), F32),
            pltpu.VMEM((rows, LANES), F32),
            pltpu.VMEM((rows, KEY_WIDTH), F32),
        ],
        compiler_params=pltpu.CompilerParams(
            dimension_semantics=("arbitrary", "arbitrary"), vmem_limit_bytes=VMEM_LIMIT),
        name="attn",
    )(cqn, keys3d, lp["w_q"], lp["w_uk"], lp["w_uv"], tabs["cq"], tabs["sq"])


def _ffn_kernel(x_ref, y_ref, o_ref, woa_ref, wob_ref, nf_ref, wg_ref, wu_ref, wd_ref, nfin_ref,
                xo_ref, *, final):
    tm = x_ref.shape[0]
    sub = min(tm, FFN_SUBTILE)
    for r0 in range(0, tm, sub):
        rows = slice(r0, r0 + sub)
        x1 = x_ref[rows, :] + _dot(y_ref[rows, :], woa_ref[...]) + _dot(o_ref[rows, :], wob_ref[...])
        hf = _rms(x1, nf_ref[...]).astype(BF16)
        g = _dot(hf, wg_ref[...])
        u = _dot(hf, wu_ref[...])
        x2 = x1 + _dot((_silu(g) * u).astype(BF16), wd_ref[...])
        if final:
            x2 = _rms(x2, nfin_ref[...])
        xo_ref[rows, :] = x2


def _ffn(x2d, y, o, lp, norm_final, final):
    n = x2d.shape[0]
    tm = min(512, n)
    assert n % tm == 0
    row = lambda i: (i, 0)
    const = lambda i: (0, 0)
    resident = functools.partial(pl.BlockSpec, index_map=const, pipeline_mode=pl.Buffered(1))
    return pl.pallas_call(
        functools.partial(_ffn_kernel, final=final),
        grid=(n // tm,),
        in_specs=[
            pl.BlockSpec((tm, D_MODEL), row),
            pl.BlockSpec((tm, SSD_WIDTH), row),
            pl.BlockSpec((tm, MLA_WIDTH), row),
            resident((SSD_WIDTH, D_MODEL)),
            resident((MLA_WIDTH, D_MODEL)),
            pl.BlockSpec((1, D_MODEL), const),
            resident((D_MODEL, D_FF)),
            resident((D_MODEL, D_FF)),
            resident((D_FF, D_MODEL)),
            pl.BlockSpec((1, D_MODEL), const),
        ],
        out_specs=pl.BlockSpec((tm, D_MODEL), row),
        out_shape=jax.ShapeDtypeStruct((n, D_MODEL), F32),
        compiler_params=pltpu.CompilerParams(
            dimension_semantics=("arbitrary",), vmem_limit_bytes=VMEM_LIMIT),
        name="ffn",
    )(x2d, y, o, lp["w_out_a"], lp["w_out_b"], lp["norm_ffn"], lp["w_gate"], lp["w_up"], lp["w_down"], norm_final)


def _swap_halves_cols(w):
    half = w.shape[-1] // 2
    return jnp.concatenate([w[..., half:], w[..., :half]], axis=-1)


def _layer_params(l, w_in, w_uq, w_uk, w_uv, w_out, norm_mix, q_norm, kv_norm, ssd_norm,
                  conv_w, conv_b, dt_bias, a_log, d_skip, norm_ffn, w_gate, w_up, w_down):
    wi = w_in[l]
    w_kr = wi[:, S_Q + KV_LORA:]
    misc = jnp.concatenate([
        wi[:, S_XBC:S_DT], jnp.zeros((D_MODEL, MISC_KR - SSD_HEADS), F32),
        w_kr, _swap_halves_cols(w_kr),
        jnp.zeros((D_MODEL, LANES - MISC_KR_SWAPPED - QK_ROPE), F32)], axis=1)
    w_in_r = jnp.concatenate([wi[:, :S_XBC], wi[:, S_DT:S_Q], wi[:, S_Q:S_Q + KV_LORA], misc], axis=1)

    wq = w_uq[l].reshape(Q_LORA, MLA_HEADS, QK_NOPE + QK_ROPE)
    wq_rope = wq[:, :, QK_NOPE:]
    w_q = jnp.concatenate([wq, _swap_halves_cols(wq_rope)], axis=-1).reshape(Q_LORA, MLA_HEADS * LANES)

    uk_t = jnp.transpose(w_uk[l], (1, 2, 0))
    eye = jnp.eye(QK_ROPE, dtype=F32)
    top = jnp.concatenate([uk_t, jnp.zeros((MLA_HEADS, QK_NOPE, KEY_WIDTH - KV_LORA), F32)], axis=2)
    mid = jnp.concatenate([jnp.zeros((QK_ROPE, KV_LORA), F32), eye,
                           jnp.zeros((QK_ROPE, KEY_WIDTH - KV_LORA - QK_ROPE), F32)], axis=1)
    mid = jnp.broadcast_to(mid[None], (MLA_HEADS, QK_ROPE, KEY_WIDTH))
    bot = jnp.zeros((MLA_HEADS, LANES - QK_NOPE - QK_ROPE, KEY_WIDTH), F32)
    w_uk_p = jnp.concatenate([top, mid, bot], axis=1)

    uv = jnp.transpose(w_uv[l], (1, 0, 2)).reshape(MLA_HEADS // 2, 2, KV_LORA, V_HEAD)
    zero = jnp.zeros((MLA_HEADS // 2, KV_LORA, V_HEAD), F32)
    w_uv_p = jnp.concatenate([jnp.concatenate([uv[:, 0], zero], axis=2),
                              jnp.concatenate([zero, uv[:, 1]], axis=2)], axis=1)

    pad_heads = lambda v: jnp.concatenate([v, jnp.zeros((LANES - SSD_HEADS,), F32)])[None]
    return {
        "w_in": w_in_r.astype(BF16), "w_q": w_q.astype(BF16), "w_uk": w_uk_p.astype(BF16),
        "w_uv": w_uv_p.astype(BF16),
        "w_out_a": w_out[l][:SSD_WIDTH].astype(BF16), "w_out_b": w_out[l][SSD_WIDTH:].astype(BF16),
        "w_gate": w_gate[l].astype(BF16), "w_up": w_up[l].astype(BF16), "w_down": w_down[l].astype(BF16),
        "norm_mix": norm_mix[l][None], "q_norm": q_norm[l][None], "kv_norm": kv_norm[l][None],
        "ssd_norm": ssd_norm[l][None], "norm_ffn": norm_ffn[l][None],
        "conv_w": conv_w[l], "conv_b": conv_b[l][None],
        "dt_bias": pad_heads(dt_bias[l]), "a_log": pad_heads(a_log[l]),
        "d_skip": jnp.repeat(d_skip[l], SSD_HEAD_DIM)[None],
    }


def _rope_tables(past, T):
    pos = (past + jnp.arange(T)).astype(F32)
    inv = 1.0 / (ROPE_BASE ** (jnp.arange(0, QK_ROPE, 2, dtype=F32) / QK_ROPE))
    ang = pos[:, None] * inv[None, :]
    cos, sin = jnp.cos(ang), jnp.sin(ang)
    c32 = jnp.concatenate([cos, cos], axis=1)
    s32 = jnp.concatenate([-sin, sin], axis=1)
    scale = (QK_NOPE + QK_ROPE) ** -0.5 * LOG2_E
    zeros = lambda w: jnp.zeros((T, w), F32)
    return {
        "ck": jnp.concatenate([c32, zeros(LANES - QK_ROPE)], axis=1),
        "sk": jnp.concatenate([s32, zeros(LANES - QK_ROPE)], axis=1),
        "cq": scale * jnp.concatenate([jnp.ones((T, QK_NOPE), F32), c32, zeros(LANES - QK_NOPE - QK_ROPE)], axis=1),
        "sq": scale * jnp.concatenate([zeros(QK_NOPE), s32, zeros(LANES - QK_NOPE - QK_ROPE)], axis=1),
    }


def _hybrid_layer(x2d, B, T, past_keys, h0, conv0, lp, tabs, norm_final, final):
    past = 0 if past_keys is None else past_keys.shape[1]
    z, xbc, cqn, ckv, kr, keys, dt = _in_proj(x2d, T, lp, tabs)
    y, h_new, conv_new = _ssd(xbc, z, dt, h0, conv0, B, T, lp)
    keys3d = keys.reshape(B, T, KEY_WIDTH)
    if past_keys is not None:
        keys3d = jnp.concatenate([past_keys, keys3d], axis=1)
    o = _attn(cqn, keys3d, B, T, past, lp, tabs)
    x_new = _ffn(x2d, y, o, lp, norm_final, final)
    return x_new, ckv, kr, h_new, conv_new


def kernel(x_prompt, x_sample, cache_mla_ckv, cache_mla_krope, state_ssm, state_conv, w_in, w_uq, w_uk, w_uv, w_out, norm_mix, q_norm, kv_norm, ssd_norm, conv_w, conv_b, dt_bias, a_log, d_skip, norm_ffn, w_gate, w_up, w_down, norm_final):
    depth = w_in.shape[0]
    bp, tp, _ = x_prompt.shape
    bs, tsm, _ = x_sample.shape
    past = cache_mla_ckv.shape[2]
    state_rows = SSD_HEADS * SSD_HEAD_DIM

    xp = x_prompt.reshape(bp * tp, D_MODEL)
    xs = x_sample.reshape(bs * tsm, D_MODEL)
    tabs_p = _rope_tables(0, tp)
    tabs_s = _rope_tables(past, tsm)
    zero_state = jnp.zeros((bp, state_rows, D_STATE), F32)
    zero_conv = jnp.zeros((bp, CONV_W - 1, CONV_DIM), F32)
    nfin = norm_final[None]

    outs_p = [[], [], [], []]
    outs_s = [[], [], [], []]
    for l in range(depth):
        lp = _layer_params(l, w_in, w_uq, w_uk, w_uv, w_out, norm_mix, q_norm, kv_norm, ssd_norm,
                           conv_w, conv_b, dt_bias, a_log, d_skip, norm_ffn, w_gate, w_up, w_down)
        final = l == depth - 1
        xp, ckv, kr, hs, cb = _hybrid_layer(xp, bp, tp, None, zero_state, zero_conv, lp, tabs_p, nfin, final)
        for acc, v in zip(outs_p, (ckv.reshape(bp, tp, KV_LORA), kr.reshape(bp, tp, QK_ROPE),
                                   hs.reshape(bp, SSD_HEADS, SSD_HEAD_DIM, D_STATE), cb)):
            acc.append(v)
        past_keys = jnp.concatenate(
            [cache_mla_ckv[l], cache_mla_krope[l],
             jnp.zeros((bs, past, KEY_WIDTH - KV_LORA - QK_ROPE), F32)], axis=-1).astype(BF16)
        xs, ckv, kr, hs, cb = _hybrid_layer(
            xs, bs, tsm, past_keys, state_ssm[l].reshape(bs, state_rows, D_STATE), state_conv[l],
            lp, tabs_s, nfin, final)
        for acc, v in zip(outs_s, (ckv.reshape(bs, tsm, KV_LORA), kr.reshape(bs, tsm, QK_ROPE),
                                   hs.reshape(bs, SSD_HEADS, SSD_HEAD_DIM, D_STATE), cb)):
            acc.append(v)
    return (xp.reshape(bp, tp, D_MODEL), xs.reshape(bs, tsm, D_MODEL),
            *(jnp.stack(v) for v in outs_p), *(jnp.stack(v) for v in outs_s))
```

```python
import functools

import jax
import jax.numpy as jnp
from jax import lax
from jax.experimental import pallas as pl
from jax.experimental.pallas import tpu as pltpu

F32 = jnp.float32
BF16 = jnp.bfloat16

D_MODEL = 1024
CHUNK = 64
EPS = 1e-6
SSD_HEADS = 8
SSD_HEAD_DIM = 64
SSD_WIDTH = SSD_HEADS * SSD_HEAD_DIM
SSD_GROUPS = 2
D_STATE = 128
CONV_W = 4
CONV_DIM = SSD_WIDTH + 2 * SSD_GROUPS * D_STATE
MLA_HEADS = 8
QK_NOPE = 64
QK_ROPE = 32
V_HEAD = 64
Q_LORA = 256
KV_LORA = 128
MLA_WIDTH = MLA_HEADS * V_HEAD
ROPE_BASE = 10000.0
LOG2_E = 1.4426950408889634
S_Z = SSD_WIDTH
S_XBC = S_Z + CONV_DIM
S_DT = S_XBC + SSD_HEADS
S_Q = S_DT + Q_LORA
D_FF = 2816

LANES = 128
SUBLANES = 8
KEY_WIDTH = 2 * LANES
PROJ_WIDTH = 2048
MISC_KR = 32
MISC_KR_SWAPPED = 64
VMEM_LIMIT = 56 * 1024 * 1024
IN_PROJ_SUBTILE = 128
FFN_SUBTILE = 256
SSD_ROWS_PER_STEP = 512

_NT = (((1,), (1,)), ((), ()))
_TN = (((0,), (0,)), ((), ()))


def _dot(a, b):
    return jnp.dot(a, b, preferred_element_type=F32)


def _rms(x, w):
    return x * lax.rsqrt(jnp.mean(x * x, axis=-1, keepdims=True) + EPS) * w


def _silu(x):
    return x * jax.nn.sigmoid(x)


def _split3(x):
    hi = x.astype(BF16)
    r = x - hi.astype(F32)
    mid = r.astype(BF16)
    lo = (r - mid.astype(F32)).astype(BF16)
    return hi, mid, lo


def _in_proj_kernel(x_ref, nw_ref, w_ref, qn_ref, kvn_ref, dtb_ref, ck_ref, sk_ref,
                    z_ref, xbc_ref, cqn_ref, ckv_ref, kr_ref, keys_ref, dt_ref):
    tm = x_ref.shape[0]
    sub = min(tm, IN_PROJ_SUBTILE)
    for r0 in range(0, tm, sub):
        rows = slice(r0, r0 + sub)
        h = _rms(x_ref[rows, :], nw_ref[...]).astype(BF16)
        z_ref[rows, :] = _dot(h, w_ref[:, 0:512])
        xbc_ref[rows, :] = _dot(h, w_ref[:, 512:1536])
        cqn_ref[rows, :] = _rms(_dot(h, w_ref[:, 1536:1792]), qn_ref[...]).astype(BF16)
        tail = _dot(h, w_ref[:, 1792:2048])
        ckv = _rms(tail[:, :LANES], kvn_ref[...])
        ckv_ref[rows, :] = ckv
        misc = tail[:, LANES:]
        kr = (pltpu.roll(misc, LANES - MISC_KR, 1) * ck_ref[rows, :]
              + pltpu.roll(misc, LANES - MISC_KR_SWAPPED, 1) * sk_ref[rows, :])
        kr_ref[rows, :] = kr[:, :QK_ROPE]
        keys_ref[rows, :] = jnp.concatenate([ckv, kr], axis=1).astype(BF16)
        lane = lax.broadcasted_iota(jnp.int32, misc.shape, 1)
        dt_ref[rows, :] = jnp.where(lane < SSD_HEADS, jax.nn.softplus(misc + dtb_ref[...]), 0.0)


def _in_proj(x2d, T, lp, tabs):
    n = x2d.shape[0]
    tm = min(512, n)
    assert n % tm == 0
    if tm <= T:
        assert T % tm == 0
        per_seq = T // tm
        ck, sk = tabs["ck"], tabs["sk"]
        tab_map = lambda i: (i % per_seq, 0)
    else:
        assert tm % T == 0
        ck = jnp.tile(tabs["ck"], (tm // T, 1))
        sk = jnp.tile(tabs["sk"], (tm // T, 1))
        tab_map = lambda i: (0, 0)
    row = lambda i: (i, 0)
    const = lambda i: (0, 0)
    return pl.pallas_call(
        _in_proj_kernel,
        grid=(n // tm,),
        in_specs=[
            pl.BlockSpec((tm, D_MODEL), row),
            pl.BlockSpec((1, D_MODEL), const),
            pl.BlockSpec((D_MODEL, PROJ_WIDTH), const),
            pl.BlockSpec((1, Q_LORA), const),
            pl.BlockSpec((1, KV_LORA), const),
            pl.BlockSpec((1, LANES), const),
            pl.BlockSpec((tm, LANES), tab_map),
            pl.BlockSpec((tm, LANES), tab_map),
        ],
        out_specs=[
            pl.BlockSpec((tm, SSD_WIDTH), row),
            pl.BlockSpec((tm, CONV_DIM), row),
            pl.BlockSpec((tm, Q_LORA), row),
            pl.BlockSpec((tm, KV_LORA), row),
            pl.BlockSpec((tm, QK_ROPE), row),
            pl.BlockSpec((tm, KEY_WIDTH), row),
            pl.BlockSpec((tm, LANES), row),
        ],
        out_shape=[
            jax.ShapeDtypeStruct((n, SSD_WIDTH), F32),
            jax.ShapeDtypeStruct((n, CONV_DIM), F32),
            jax.ShapeDtypeStruct((n, Q_LORA), BF16),
            jax.ShapeDtypeStruct((n, KV_LORA), F32),
            jax.ShapeDtypeStruct((n, QK_ROPE), F32),
            jax.ShapeDtypeStruct((n, KEY_WIDTH), BF16),
            jax.ShapeDtypeStruct((n, LANES), F32),
        ],
        compiler_params=pltpu.CompilerParams(
            dimension_semantics=("arbitrary",), vmem_limit_bytes=VMEM_LIMIT),
        name="in_proj",
    )(x2d, lp["norm_mix"], lp["w_in"], lp["q_norm"], lp["kv_norm"], lp["dt_bias"], ck, sk)


def _ssd_kernel(xbc_ref, z_ref, dt_ref, h0_ref, cb0_ref, cw_ref, cbias_ref, alog_ref, dskip_ref, nw_ref,
                y_ref, hout_ref, cout_ref, st_ref, ubuf_ref, act_ref, *, ts, nb):
    j = pl.program_id(1)
    pad = SUBLANES

    @pl.when(j == 0)
    def _():
        st_ref[...] = h0_ref[...]
        ubuf_ref[:, 0:pad, :] = jnp.zeros((nb, pad, CONV_DIM), F32)
        ubuf_ref[:, pad - (CONV_W - 1):pad, :] = cb0_ref[...]

    for bb in range(nb):
        ubuf_ref[bb, pad:pad + ts, :] = xbc_ref[bb]
        cur = ubuf_ref[bb, pad:pad + ts, :].reshape(ts // SUBLANES, SUBLANES, CONV_DIM)
        above = ubuf_ref[bb, 0:ts, :].reshape(ts // SUBLANES, SUBLANES, CONV_DIM)
        sub3 = lax.broadcasted_iota(jnp.int32, cur.shape, 1)
        yc = cbias_ref[...].reshape(1, 1, CONV_DIM)
        for k in range(CONV_W):
            d = CONV_W - 1 - k
            u = cur if d == 0 else pltpu.roll(jnp.where(sub3 >= SUBLANES - d, above, cur), d, 1)
            yc = yc + u * cw_ref[k:k + 1, :].reshape(1, 1, CONV_DIM)
        act_ref[bb] = _silu(yc).reshape(ts, CONV_DIM)
        cout_ref[bb] = ubuf_ref[bb, pad + ts - (CONV_W - 1):pad + ts, :]
        ubuf_ref[bb, 0:pad, :] = ubuf_ref[bb, ts:ts + pad, :]

    L = CHUNK
    lane = lax.broadcasted_iota(jnp.int32, (L, LANES), 1)
    sub = lax.broadcasted_iota(jnp.int32, (L, LANES), 0)
    left = lane < L
    tril2 = (lane % L) <= sub
    r64 = lax.broadcasted_iota(jnp.int32, (L, L), 0)
    c64 = lax.broadcasted_iota(jnp.int32, (L, L), 1)
    tril = (c64 <= r64).astype(BF16)
    sel8 = (lax.broadcasted_iota(jnp.int32, (8, LANES), 0)
            == lax.broadcasted_iota(jnp.int32, (8, LANES), 1)).astype(BF16)
    head_lane = lax.broadcasted_iota(jnp.int32, (1, LANES), 1) < SSD_HEADS
    a_neg = jnp.where(head_lane, -jnp.exp(alog_ref[...]), 0.0)

    def pair_rows(v):
        stacked = jnp.concatenate([v, pltpu.roll(v, LANES - 1, 1)], axis=0)
        out = jnp.zeros((8, LANES), F32)
        for part in _split3(stacked):
            out = out + lax.dot_general(sel8, part, _NT, preferred_element_type=F32)
        return out

    def pair_cols(v, k):
        return jnp.where(left, v[:, 2 * k:2 * k + 1], v[:, 2 * k + 1:2 * k + 2])

    pairs_per_group = SSD_HEADS // SSD_GROUPS // 2
    pairs = range(SSD_HEADS // 2)
    insts = [(c, bb) for c in range(ts // L) for bb in range(nb)]

    def pair_slice(k):
        return slice(2 * k * SSD_HEAD_DIM, (2 * k + 2) * SSD_HEAD_DIM)

    dtc, cum = {}, {}
    for c, bb in insts:
        dtc[c, bb] = dt_ref[bb, pl.ds(c * L, L), :]
        acc = jnp.zeros((L, LANES), F32)
        for part in _split3(dtc[c, bb] * a_neg):
            acc = acc + _dot(tril, part)
        cum[c, bb] = acc
    cum_rows = {i: pair_rows(cum[i]) for i in insts}

    b_g, c_g, cb2 = {}, {}, {}
    for c, bb in insts:
        for g in range(SSD_GROUPS):
            b_lo = SSD_WIDTH + g * D_STATE
            c_lo = SSD_WIDTH + (SSD_GROUPS + g) * D_STATE
            b_g[c, bb, g] = act_ref[bb, pl.ds(c * L, L), b_lo:b_lo + D_STATE].astype(BF16)
            c_g[c, bb, g] = act_ref[bb, pl.ds(c * L, L), c_lo:c_lo + D_STATE].astype(BF16)
            cb2[c, bb, g] = lax.dot_general(c_g[c, bb, g], jnp.concatenate([b_g[c, bb, g]] * 2, axis=0), _NT,
                                            preferred_element_type=F32)

    y_in, upd, exp_ccol, dec2 = {}, {}, {}, {}
    for c, bb in insts:
        last = cum[c, bb][L - 1:L, :]
        exp_last = jnp.exp(last)
        for k in pairs:
            g = k // pairs_per_group
            ccol = pair_cols(cum[c, bb], k)
            dtcol = pair_cols(dtc[c, bb], k)
            crow = cum_rows[c, bb][2 * k:2 * k + 1, :]
            decay = jnp.exp(jnp.where(tril2, ccol - crow, -jnp.inf))
            w2 = (decay * cb2[c, bb, g]).astype(BF16)
            xdt = act_ref[bb, pl.ds(c * L, L), pair_slice(k)] * dtcol
            xbd = jnp.concatenate([jnp.where(left, xdt, 0.0), jnp.where(left, 0.0, xdt)], axis=0).astype(BF16)
            y_in[c, bb, k] = _dot(w2, xbd)
            last2 = jnp.where(left[0:1, :], last[:, 2 * k:2 * k + 1], last[:, 2 * k + 1:2 * k + 2])
            xw = (xdt * jnp.exp(last2 - ccol)).astype(BF16)
            upd[c, bb, k] = lax.dot_general(xw, b_g[c, bb, g], _TN, preferred_element_type=F32)
            exp_ccol[c, bb, k] = jnp.exp(ccol)
            dec2[c, bb, k] = jnp.concatenate(
                [jnp.broadcast_to(exp_last[:, 2 * k:2 * k + 1], (SSD_HEAD_DIM, D_STATE)),
                 jnp.broadcast_to(exp_last[:, 2 * k + 1:2 * k + 2], (SSD_HEAD_DIM, D_STATE))], axis=0)

    st = {(bb, k): st_ref[bb, pair_slice(k), :] for bb in range(nb) for k in pairs}
    y_st = {}
    for c, bb in insts:
        for k in pairs:
            g = k // pairs_per_group
            y_st[c, bb, k] = lax.dot_general(c_g[c, bb, g], st[bb, k].astype(BF16), _NT,
                                             preferred_element_type=F32)
            st[bb, k] = dec2[c, bb, k] * st[bb, k] + upd[c, bb, k]
    for bb in range(nb):
        for k in pairs:
            st_ref[bb, pair_slice(k), :] = st[bb, k]

    for c, bb in insts:
        xs = act_ref[bb, pl.ds(c * L, L), 0:SSD_WIDTH]
        y = jnp.concatenate([y_in[c, bb, k] + exp_ccol[c, bb, k] * y_st[c, bb, k] for k in pairs], axis=1)
        y = (y + dskip_ref[...] * xs) * _silu(z_ref[bb, pl.ds(c * L, L), :])
        y_ref[bb, pl.ds(c * L, L), :] = _rms(y, nw_ref[...]).astype(BF16)

    @pl.when(j == pl.num_programs(1) - 1)
    def _():
        hout_ref[...] = st_ref[...]


def _ssd(xbc, z, dt, h0, conv0, B, T, lp):
    ts = min(SSD_ROWS_PER_STEP, T)
    nb = min(B, SSD_ROWS_PER_STEP // ts)
    assert T % ts == 0 and ts % CHUNK == 0 and B % nb == 0 and T >= CONV_W - 1
    nt = T // ts
    row = lambda b, j: (b, j, 0)
    const = lambda b, j: (0, 0)
    per_b = lambda b, j: (b, 0, 0)
    state_rows = SSD_HEADS * SSD_HEAD_DIM
    return pl.pallas_call(
        functools.partial(_ssd_kernel, ts=ts, nb=nb),
        grid=(B // nb, nt),
        in_specs=[
            pl.BlockSpec((nb, ts, CONV_DIM), row),
            pl.BlockSpec((nb, ts, SSD_WIDTH), row),
            pl.BlockSpec((nb, ts, LANES), row),
            pl.BlockSpec((nb, state_rows, D_STATE), per_b),
            pl.BlockSpec((nb, CONV_W - 1, CONV_DIM), per_b),
            pl.BlockSpec((CONV_W, CONV_DIM), const),
            pl.BlockSpec((1, CONV_DIM), const),
            pl.BlockSpec((1, LANES), const),
            pl.BlockSpec((1, SSD_WIDTH), const),
            pl.BlockSpec((1, SSD_WIDTH), const),
        ],
        out_specs=[
            pl.BlockSpec((nb, ts, SSD_WIDTH), row),
            pl.BlockSpec((nb, state_rows, D_STATE), per_b),
            pl.BlockSpec((nb, CONV_W - 1, CONV_DIM), per_b),
        ],
        out_shape=[
            jax.ShapeDtypeStruct((B, T, SSD_WIDTH), BF16),
            jax.ShapeDtypeStruct((B, state_rows, D_STATE), F32),
            jax.ShapeDtypeStruct((B, CONV_W - 1, CONV_DIM), F32),
        ],
        scratch_shapes=[
            pltpu.VMEM((nb, state_rows, D_STATE), F32),
            pltpu.VMEM((nb, ts + SUBLANES, CONV_DIM), F32),
            pltpu.VMEM((nb, ts, CONV_DIM), F32),
        ],
        compiler_params=pltpu.CompilerParams(
            dimension_semantics=("arbitrary", "arbitrary"), vmem_limit_bytes=VMEM_LIMIT),
        name="ssd",
    )(xbc.reshape(B, T, CONV_DIM), z.reshape(B, T, SSD_WIDTH), dt.reshape(B, T, LANES), h0, conv0,
      lp["conv_w"], lp["conv_b"], lp["a_log"], lp["d_skip"], lp["ssd_norm"])


def _attn_kernel(cqn_ref, keys_ref, wq_ref, wuk_ref, wuv_ref, cq_ref, sq_ref, o_ref,
                 q_scr, v_scr, s_scr, m_scr, acc_scr, *, tq, tk, past, n_masked):
    i = pl.program_id(1)
    rows = MLA_HEADS * tq
    lane_tiles = tk // LANES

    @pl.when(i == 0)
    def _():
        kv = keys_ref[...]
        lane = lax.broadcasted_iota(jnp.int32, kv.shape, 1)
        v_scr[...] = jnp.where(lane < KV_LORA, kv, jnp.where(lane == KV_LORA, 1.0, 0.0).astype(BF16))

    qall = _dot(cqn_ref[...], wq_ref[...])
    for h in range(MLA_HEADS):
        t = qall[:, h * LANES:(h + 1) * LANES]
        r = t * cq_ref[...] + pltpu.roll(t, LANES - QK_ROPE, 1) * sq_ref[...]
        q_scr[h * tq:(h + 1) * tq, :] = _dot(r.astype(BF16), wuk_ref[h]).astype(BF16)
    m_scr[...] = jnp.full(m_scr.shape, -jnp.inf, F32)

    q_first = past + i * tq
    n_full = q_first // tk

    def for_each_run(first, count, fn):
        quads = count // 4

        def quad(t, carry):
            fn(first + 4 * t, 4)
            return carry

        lax.fori_loop(0, quads, quad, 0)
        start = first + 4 * quads
        for width in (2, 1):
            take = (count & width) != 0

            @pl.when(take)
            def _():
                fn(start, width)

            start = start + jnp.where(take, width, 0)

    def scores(jt, width, masked=False):
        k0 = pl.multiple_of(jt * tk, tk)
        s = lax.dot_general(q_scr[...], keys_ref[pl.ds(k0, width * tk), :], _NT,
                            preferred_element_type=F32)
        if masked:
            qc = (q_first + lax.broadcasted_iota(jnp.int32, (tq, width * tk), 0)) // CHUNK
            kc = (k0 + lax.broadcasted_iota(jnp.int32, (tq, width * tk), 1)) // CHUNK
            bias = jnp.where(kc <= qc, 0.0, -jnp.inf).astype(F32)
            s = (s.reshape(MLA_HEADS, tq, width * tk) + bias[None]).reshape(rows, width * tk)
        m = m_scr[...]
        for w in range(width):
            sw = s[:, w * tk:(w + 1) * tk]
            s_scr[jt + w] = sw
            for c in range(lane_tiles):
                m = jnp.maximum(m, sw[:, c * LANES:(c + 1) * LANES])
            if tk % LANES:
                m = jnp.maximum(m, jnp.max(sw[:, lane_tiles * LANES:], axis=-1, keepdims=True))
        m_scr[...] = m

    scores(n_full, n_masked, masked=True)
    for_each_run(0, n_full, scores)

    m_row = jnp.max(m_scr[...], axis=-1, keepdims=True)
    m_scr[...] = jnp.broadcast_to(m_row, m_scr.shape)

    def values(jt, width, first=False):
        k0 = pl.multiple_of(jt * tk, tk)
        mb = m_scr[...]
        if tk % LANES:
            mb = mb[:, 0:1]
        else:
            mb = jnp.concatenate([mb] * lane_tiles, axis=1)
        p = jnp.concatenate([jnp.exp2(s_scr[jt + w] - mb).astype(BF16) for w in range(width)], axis=1)
        pv = _dot(p, v_scr[pl.ds(k0, width * tk), :])
        if first:
            acc_scr[...] = pv
        else:
            acc_scr[...] += pv

    values(n_full, n_masked, first=True)
    for_each_run(0, n_full, values)

    acc = acc_scr[...]
    o_lat = acc[:, :KV_LORA] / acc[:, KV_LORA:KV_LORA + 1]
    for k in range(MLA_HEADS // 2):
        pair = jnp.concatenate([o_lat[(2 * k) * tq:(2 * k + 1) * tq, :],
                                o_lat[(2 * k + 1) * tq:(2 * k + 2) * tq, :]], axis=1).astype(BF16)
        o_ref[:, k * LANES:(k + 1) * LANES] = _dot(pair, wuv_ref[k]).astype(BF16)


def _attn(cqn, keys3d, B, T, past, lp, tabs):
    S = keys3d.shape[1]
    assert S == past + T
    if T >= 256:
        tq = tk = 256
        assert T % tq == 0 and past % tk == 0
        n_masked = tq // tk
    else:
        tq, tk = T, S
        n_masked = 1
    nq = T // tq
    rows = MLA_HEADS * tq
    const2 = lambda b, i: (0, 0)
    const3 = lambda b, i: (0, 0, 0)
    return pl.pallas_call(
        functools.partial(_attn_kernel, tq=tq, tk=tk, past=past, n_masked=n_masked),
        grid=(B, nq),
        in_specs=[
            pl.BlockSpec((tq, Q_LORA), lambda b, i: (b * nq + i, 0)),
            pl.BlockSpec((None, S, KEY_WIDTH), lambda b, i: (b, 0, 0)),
            pl.BlockSpec((Q_LORA, MLA_HEADS * LANES), const2),
            pl.BlockSpec((MLA_HEADS, LANES, KEY_WIDTH), const3),
            pl.BlockSpec((MLA_HEADS // 2, 2 * KV_LORA, LANES), const3),
            pl.BlockSpec((tq, LANES), lambda b, i: (i, 0)),
            pl.BlockSpec((tq, LANES), lambda b, i: (i, 0)),
        ],
        out_specs=pl.BlockSpec((tq, MLA_WIDTH), lambda b, i: (b * nq + i, 0)),
        out_shape=jax.ShapeDtypeStruct((B * T, MLA_WIDTH), BF16),
        scratch_shapes=[
            pltpu.VMEM((rows, KEY_WIDTH), BF16),
            pltpu.VMEM((S, KEY_WIDTH), BF16),
            pltpu.VMEM((S // tk, rows, tk), F32),
            pltpu.VMEM((rows, LANES), F32),
            pltpu.VMEM((rows, KEY_WIDTH), F32),
        ],
        compiler_params=pltpu.CompilerParams(
            dimension_semantics=("arbitrary", "arbitrary"), vmem_limit_bytes=VMEM_LIMIT),
        name="attn",
    )(cqn, keys3d, lp["w_q"], lp["w_uk"], lp["w_uv"], tabs["cq"], tabs["sq"])


def _ffn_kernel(x_ref, y_ref, o_ref, woa_ref, wob_ref, nf_ref, wg_ref, wu_ref, wd_ref, nfin_ref,
                xo_ref, *, final):
    tm = x_ref.shape[0]
    sub = min(tm, FFN_SUBTILE)
    for r0 in range(0, tm, sub):
        rows = slice(r0, r0 + sub)
        x1 = x_ref[rows, :] + _dot(y_ref[rows, :], woa_ref[...]) + _dot(o_ref[rows, :], wob_ref[...])
        hf = _rms(x1, nf_ref[...]).astype(BF16)
        g = _dot(hf, wg_ref[...])
        u = _dot(hf, wu_ref[...])
        x2 = x1 + _dot((_silu(g) * u).astype(BF16), wd_ref[...])
        if final:
            x2 = _rms(x2, nfin_ref[...])
        xo_ref[rows, :] = x2


def _ffn(x2d, y, o, lp, norm_final, final):
    n = x2d.shape[0]
    tm = min(512, n)
    assert n % tm == 0
    row = lambda i: (i, 0)
    const = lambda i: (0, 0)
    resident = functools.partial(pl.BlockSpec, index_map=const, pipeline_mode=pl.Buffered(1))
    return pl.pallas_call(
        functools.partial(_ffn_kernel, final=final),
        grid=(n // tm,),
        in_specs=[
            pl.BlockSpec((tm, D_MODEL), row),
            pl.BlockSpec((tm, SSD_WIDTH), row),
            pl.BlockSpec((tm, MLA_WIDTH), row),
            resident((SSD_WIDTH, D_MODEL)),
            resident((MLA_WIDTH, D_MODEL)),
            pl.BlockSpec((1, D_MODEL), const),
            resident((D_MODEL, D_FF)),
            resident((D_MODEL, D_FF)),
            resident((D_FF, D_MODEL)),
            pl.BlockSpec((1, D_MODEL), const),
        ],
        out_specs=pl.BlockSpec((tm, D_MODEL), row),
        out_shape=jax.ShapeDtypeStruct((n, D_MODEL), F32),
        compiler_params=pltpu.CompilerParams(
            dimension_semantics=("arbitrary",), vmem_limit_bytes=VMEM_LIMIT),
        name="ffn",
    )(x2d, y, o, lp["w_out_a"], lp["w_out_b"], lp["norm_ffn"], lp["w_gate"], lp["w_up"], lp["w_down"], norm_final)


def _swap_halves_cols(w):
    half = w.shape[-1] // 2
    return jnp.concatenate([w[..., half:], w[..., :half]], axis=-1)


def _layer_params(l, w_in, w_uq, w_uk, w_uv, w_out, norm_mix, q_norm, kv_norm, ssd_norm,
                  conv_w, conv_b, dt_bias, a_log, d_skip, norm_ffn, w_gate, w_up, w_down):
    wi = w_in[l]
    w_kr = wi[:, S_Q + KV_LORA:]
    misc = jnp.concatenate([
        wi[:, S_XBC:S_DT], jnp.zeros((D_MODEL, MISC_KR - SSD_HEADS), F32),
        w_kr, _swap_halves_cols(w_kr),
        jnp.zeros((D_MODEL, LANES - MISC_KR_SWAPPED - QK_ROPE), F32)], axis=1)
    w_in_r = jnp.concatenate([wi[:, :S_XBC], wi[:, S_DT:S_Q], wi[:, S_Q:S_Q + KV_LORA], misc], axis=1)

    wq = w_uq[l].reshape(Q_LORA, MLA_HEADS, QK_NOPE + QK_ROPE)
    wq_rope = wq[:, :, QK_NOPE:]
    w_q = jnp.concatenate([wq, _swap_halves_cols(wq_rope)], axis=-1).reshape(Q_LORA, MLA_HEADS * LANES)

    uk_t = jnp.transpose(w_uk[l], (1, 2, 0))
    eye = jnp.eye(QK_ROPE, dtype=F32)
    top = jnp.concatenate([uk_t, jnp.zeros((MLA_HEADS, QK_NOPE, KEY_WIDTH - KV_LORA), F32)], axis=2)
    mid = jnp.concatenate([jnp.zeros((QK_ROPE, KV_LORA), F32), eye,
                           jnp.zeros((QK_ROPE, KEY_WIDTH - KV_LORA - QK_ROPE), F32)], axis=1)
    mid = jnp.broadcast_to(mid[None], (MLA_HEADS, QK_ROPE, KEY_WIDTH))
    bot = jnp.zeros((MLA_HEADS, LANES - QK_NOPE - QK_ROPE, KEY_WIDTH), F32)
    w_uk_p = jnp.concatenate([top, mid, bot], axis=1)

    uv = jnp.transpose(w_uv[l], (1, 0, 2)).reshape(MLA_HEADS // 2, 2, KV_LORA, V_HEAD)
    zero = jnp.zeros((MLA_HEADS // 2, KV_LORA, V_HEAD), F32)
    w_uv_p = jnp.concatenate([jnp.concatenate([uv[:, 0], zero], axis=2),
                              jnp.concatenate([zero, uv[:, 1]], axis=2)], axis=1)

    pad_heads = lambda v: jnp.concatenate([v, jnp.zeros((LANES - SSD_HEADS,), F32)])[None]
    return {
        "w_in": w_in_r.astype(BF16), "w_q": w_q.astype(BF16), "w_uk": w_uk_p.astype(BF16),
        "w_uv": w_uv_p.astype(BF16),
        "w_out_a": w_out[l][:SSD_WIDTH].astype(BF16), "w_out_b": w_out[l][SSD_WIDTH:].astype(BF16),
        "w_gate": w_gate[l].astype(BF16), "w_up": w_up[l].astype(BF16), "w_down": w_down[l].astype(BF16),
        "norm_mix": norm_mix[l][None], "q_norm": q_norm[l][None], "kv_norm": kv_norm[l][None],
        "ssd_norm": ssd_norm[l][None], "norm_ffn": norm_ffn[l][None],
        "conv_w": conv_w[l], "conv_b": conv_b[l][None],
        "dt_bias": pad_heads(dt_bias[l]), "a_log": pad_heads(a_log[l]),
        "d_skip": jnp.repeat(d_skip[l], SSD_HEAD_DIM)[None],
    }


def _rope_tables(past, T):
    pos = (past + jnp.arange(T)).astype(F32)
    inv = 1.0 / (ROPE_BASE ** (jnp.arange(0, QK_ROPE, 2, dtype=F32) / QK_ROPE))
    ang = pos[:, None] * inv[None, :]
    cos, sin = jnp.cos(ang), jnp.sin(ang)
    c32 = jnp.concatenate([cos, cos], axis=1)
    s32 = jnp.concatenate([-sin, sin], axis=1)
    scale = (QK_NOPE + QK_ROPE) ** -0.5 * LOG2_E
    zeros = lambda w: jnp.zeros((T, w), F32)
    return {
        "ck": jnp.concatenate([c32, zeros(LANES - QK_ROPE)], axis=1),
        "sk": jnp.concatenate([s32, zeros(LANES - QK_ROPE)], axis=1),
        "cq": scale * jnp.concatenate([jnp.ones((T, QK_NOPE), F32), c32, zeros(LANES - QK_NOPE - QK_ROPE)], axis=1),
        "sq": scale * jnp.concatenate([zeros(QK_NOPE), s32, zeros(LANES - QK_NOPE - QK_ROPE)], axis=1),
    }


def _hybrid_layer(x2d, B, T, past_keys, h0, conv0, lp, tabs, norm_final, final):
    past = 0 if past_keys is None else past_keys.shape[1]
    z, xbc, cqn, ckv, kr, keys, dt = _in_proj(x2d, T, lp, tabs)
    y, h_new, conv_new = _ssd(xbc, z, dt, h0, conv0, B, T, lp)
    y = y.reshape(B * T, SSD_WIDTH)
    keys3d = keys.reshape(B, T, KEY_WIDTH)
    if past_keys is not None:
        keys3d = jnp.concatenate([past_keys, keys3d], axis=1)
    o = _attn(cqn, keys3d, B, T, past, lp, tabs)
    x_new = _ffn(x2d, y, o, lp, norm_final, final)
    return x_new, ckv, kr, h_new, conv_new


def kernel(x_prompt, x_sample, cache_mla_ckv, cache_mla_krope, state_ssm, state_conv, w_in, w_uq, w_uk, w_uv, w_out, norm_mix, q_norm, kv_norm, ssd_norm, conv_w, conv_b, dt_bias, a_log, d_skip, norm_ffn, w_gate, w_up, w_down, norm_final):
    depth = w_in.shape[0]
    bp, tp, _ = x_prompt.shape
    bs, tsm, _ = x_sample.shape
    past = cache_mla_ckv.shape[2]
    state_rows = SSD_HEADS * SSD_HEAD_DIM

    xp = x_prompt.reshape(bp * tp, D_MODEL)
    xs = x_sample.reshape(bs * tsm, D_MODEL)
    tabs_p = _rope_tables(0, tp)
    tabs_s = _rope_tables(past, tsm)
    zero_state = jnp.zeros((bp, state_rows, D_STATE), F32)
    zero_conv = jnp.zeros((bp, CONV_W - 1, CONV_DIM), F32)
    nfin = norm_final[None]

    outs_p = [[], [], [], []]
    outs_s = [[], [], [], []]
    for l in range(depth):
        lp = _layer_params(l, w_in, w_uq, w_uk, w_uv, w_out, norm_mix, q_norm, kv_norm, ssd_norm,
                           conv_w, conv_b, dt_bias, a_log, d_skip, norm_ffn, w_gate, w_up, w_down)
        final = l == depth - 1
        xp, ckv, kr, hs, cb = _hybrid_layer(xp, bp, tp, None, zero_state, zero_conv, lp, tabs_p, nfin, final)
        for acc, v in zip(outs_p, (ckv.reshape(bp, tp, KV_LORA), kr.reshape(bp, tp, QK_ROPE),
                                   hs.reshape(bp, SSD_HEADS, SSD_HEAD_DIM, D_STATE), cb)):
            acc.append(v)
        past_keys = jnp.concatenate(
            [cache_mla_ckv[l], cache_mla_krope[l],
             jnp.zeros((bs, past, KEY_WIDTH - KV_LORA - QK_ROPE), F32)], axis=-1).astype(BF16)
        xs, ckv, kr, hs, cb = _hybrid_layer(
            xs, bs, tsm, past_keys, state_ssm[l].reshape(bs, state_rows, D_STATE), state_conv[l],
            lp, tabs_s, nfin, final)
        for acc, v in zip(outs_s, (ckv.reshape(bs, tsm, KV_LORA), kr.reshape(bs, tsm, QK_ROPE),
                                   hs.reshape(bs, SSD_HEADS, SSD_HEAD_DIM, D_STATE), cb)):
            acc.append(v)
    return (xp.reshape(bp, tp, D_MODEL), xs.reshape(bs, tsm, D_MODEL),
            *(jnp.stack(v) for v in outs_p), *(jnp.stack(v) for v in outs_s))
```

```python
import functools

import jax
import jax.numpy as jnp
from jax import lax
from jax.experimental import pallas as pl
from jax.experimental.pallas import tpu as pltpu

F32 = jnp.float32
BF16 = jnp.bfloat16

D_MODEL = 1024
CHUNK = 64
EPS = 1e-6
SSD_HEADS = 8
SSD_HEAD_DIM = 64
SSD_WIDTH = SSD_HEADS * SSD_HEAD_DIM
SSD_GROUPS = 2
D_STATE = 128
CONV_W = 4
CONV_DIM = SSD_WIDTH + 2 * SSD_GROUPS * D_STATE
MLA_HEADS = 8
QK_NOPE = 64
QK_ROPE = 32
V_HEAD = 64
Q_LORA = 256
KV_LORA = 128
MLA_WIDTH = MLA_HEADS * V_HEAD
ROPE_BASE = 10000.0
LOG2_E = 1.4426950408889634
S_Z = SSD_WIDTH
S_XBC = S_Z + CONV_DIM
S_DT = S_XBC + SSD_HEADS
S_Q = S_DT + Q_LORA
D_FF = 2816

LANES = 128
SUBLANES = 8
KEY_WIDTH = 2 * LANES
PROJ_WIDTH = 2048
MISC_KR = 32
MISC_KR_SWAPPED = 64
VMEM_LIMIT = 56 * 1024 * 1024
IN_PROJ_SUBTILE = 128
FFN_SUBTILE = 256
SSD_ROWS_PER_STEP = 1024

_NT = (((1,), (1,)), ((), ()))
_TN = (((0,), (0,)), ((), ()))


def _dot(a, b):
    return jnp.dot(a, b, preferred_element_type=F32)


def _rms(x, w):
    return x * lax.rsqrt(jnp.mean(x * x, axis=-1, keepdims=True) + EPS) * w


def _silu(x):
    return x * jax.nn.sigmoid(x)


def _split3(x):
    hi = x.astype(BF16)
    r = x - hi.astype(F32)
    mid = r.astype(BF16)
    lo = (r - mid.astype(F32)).astype(BF16)
    return hi, mid, lo


def _in_proj_kernel(x_ref, nw_ref, w_ref, qn_ref, kvn_ref, dtb_ref, ck_ref, sk_ref,
                    z_ref, xbc_ref, cqn_ref, ckv_ref, kr_ref, keys_ref, dt_ref):
    tm = x_ref.shape[0]
    sub = min(tm, IN_PROJ_SUBTILE)
    for r0 in range(0, tm, sub):
        rows = slice(r0, r0 + sub)
        h = _rms(x_ref[rows, :], nw_ref[...]).astype(BF16)
        z_ref[rows, :] = _dot(h, w_ref[:, 0:512])
        xbc_ref[rows, :] = _dot(h, w_ref[:, 512:1536])
        cqn_ref[rows, :] = _rms(_dot(h, w_ref[:, 1536:1792]), qn_ref[...]).astype(BF16)
        tail = _dot(h, w_ref[:, 1792:2048])
        ckv = _rms(tail[:, :LANES], kvn_ref[...])
        ckv_ref[rows, :] = ckv
        misc = tail[:, LANES:]
        kr = (pltpu.roll(misc, LANES - MISC_KR, 1) * ck_ref[rows, :]
              + pltpu.roll(misc, LANES - MISC_KR_SWAPPED, 1) * sk_ref[rows, :])
        kr_ref[rows, :] = kr[:, :QK_ROPE]
        keys_ref[rows, :] = jnp.concatenate([ckv, kr], axis=1).astype(BF16)
        lane = lax.broadcasted_iota(jnp.int32, misc.shape, 1)
        dt_ref[rows, :] = jnp.where(lane < SSD_HEADS, jax.nn.softplus(misc + dtb_ref[...]), 0.0)


def _in_proj(x2d, T, lp, tabs):
    n = x2d.shape[0]
    tm = min(512, n)
    assert n % tm == 0
    if tm <= T:
        assert T % tm == 0
        per_seq = T // tm
        ck, sk = tabs["ck"], tabs["sk"]
        tab_map = lambda i: (i % per_seq, 0)
    else:
        assert tm % T == 0
        ck = jnp.tile(tabs["ck"], (tm // T, 1))
        sk = jnp.tile(tabs["sk"], (tm // T, 1))
        tab_map = lambda i: (0, 0)
    row = lambda i: (i, 0)
    const = lambda i: (0, 0)
    return pl.pallas_call(
        _in_proj_kernel,
        grid=(n // tm,),
        in_specs=[
            pl.BlockSpec((tm, D_MODEL), row),
            pl.BlockSpec((1, D_MODEL), const),
            pl.BlockSpec((D_MODEL, PROJ_WIDTH), const),
            pl.BlockSpec((1, Q_LORA), const),
            pl.BlockSpec((1, KV_LORA), const),
            pl.BlockSpec((1, LANES), const),
            pl.BlockSpec((tm, LANES), tab_map),
            pl.BlockSpec((tm, LANES), tab_map),
        ],
        out_specs=[
            pl.BlockSpec((tm, SSD_WIDTH), row),
            pl.BlockSpec((tm, CONV_DIM), row),
            pl.BlockSpec((tm, Q_LORA), row),
            pl.BlockSpec((tm, KV_LORA), row),
            pl.BlockSpec((tm, QK_ROPE), row),
            pl.BlockSpec((tm, KEY_WIDTH), row),
            pl.BlockSpec((tm, LANES), row),
        ],
        out_shape=[
            jax.ShapeDtypeStruct((n, SSD_WIDTH), F32),
            jax.ShapeDtypeStruct((n, CONV_DIM), F32),
            jax.ShapeDtypeStruct((n, Q_LORA), BF16),
            jax.ShapeDtypeStruct((n, KV_LORA), F32),
            jax.ShapeDtypeStruct((n, QK_ROPE), F32),
            jax.ShapeDtypeStruct((n, KEY_WIDTH), BF16),
            jax.ShapeDtypeStruct((n, LANES), F32),
        ],
        compiler_params=pltpu.CompilerParams(
            dimension_semantics=("arbitrary",), vmem_limit_bytes=VMEM_LIMIT),
        name="in_proj",
    )(x2d, lp["norm_mix"], lp["w_in"], lp["q_norm"], lp["kv_norm"], lp["dt_bias"], ck, sk)


def _ssd_kernel(xbc_ref, z_ref, dt_ref, h0_ref, cb0_ref, cw_ref, cbias_ref, alog_ref, dskip_ref, nw_ref,
                y_ref, hout_ref, cout_ref, st_ref, ubuf_ref, act_ref, *, ts, nb):
    j = pl.program_id(1)
    pad = SUBLANES

    @pl.when(j == 0)
    def _():
        st_ref[...] = h0_ref[...]
        ubuf_ref[:, 0:pad, :] = jnp.zeros((nb, pad, CONV_DIM), F32)
        ubuf_ref[:, pad - (CONV_W - 1):pad, :] = cb0_ref[...]

    for bb in range(nb):
        ubuf_ref[bb, pad:pad + ts, :] = xbc_ref[bb]
        cur = ubuf_ref[bb, pad:pad + ts, :].reshape(ts // SUBLANES, SUBLANES, CONV_DIM)
        above = ubuf_ref[bb, 0:ts, :].reshape(ts // SUBLANES, SUBLANES, CONV_DIM)
        sub3 = lax.broadcasted_iota(jnp.int32, cur.shape, 1)
        yc = cbias_ref[...].reshape(1, 1, CONV_DIM)
        for k in range(CONV_W):
            d = CONV_W - 1 - k
            u = cur if d == 0 else pltpu.roll(jnp.where(sub3 >= SUBLANES - d, above, cur), d, 1)
            yc = yc + u * cw_ref[k:k + 1, :].reshape(1, 1, CONV_DIM)
        act_ref[bb] = _silu(yc).reshape(ts, CONV_DIM)
        cout_ref[bb] = ubuf_ref[bb, pad + ts - (CONV_W - 1):pad + ts, :]
        ubuf_ref[bb, 0:pad, :] = ubuf_ref[bb, ts:ts + pad, :]

    L = CHUNK
    lane = lax.broadcasted_iota(jnp.int32, (L, LANES), 1)
    sub = lax.broadcasted_iota(jnp.int32, (L, LANES), 0)
    left = lane < L
    tril2 = (lane % L) <= sub
    r64 = lax.broadcasted_iota(jnp.int32, (L, L), 0)
    c64 = lax.broadcasted_iota(jnp.int32, (L, L), 1)
    tril = (c64 <= r64).astype(BF16)
    sel8 = (lax.broadcasted_iota(jnp.int32, (8, LANES), 0)
            == lax.broadcasted_iota(jnp.int32, (8, LANES), 1)).astype(BF16)
    head_lane = lax.broadcasted_iota(jnp.int32, (1, LANES), 1) < SSD_HEADS
    a_neg = jnp.where(head_lane, -jnp.exp(alog_ref[...]), 0.0)

    def pair_rows(v):
        stacked = jnp.concatenate([v, pltpu.roll(v, LANES - 1, 1)], axis=0)
        out = jnp.zeros((8, LANES), F32)
        for part in _split3(stacked):
            out = out + lax.dot_general(sel8, part, _NT, preferred_element_type=F32)
        return out

    def pair_cols(v, k):
        return jnp.where(left, v[:, 2 * k:2 * k + 1], v[:, 2 * k + 1:2 * k + 2])

    pairs_per_group = SSD_HEADS // SSD_GROUPS // 2
    pairs = range(SSD_HEADS // 2)
    insts = [(c, bb) for c in range(ts // L) for bb in range(nb)]

    def pair_slice(k):
        return slice(2 * k * SSD_HEAD_DIM, (2 * k + 2) * SSD_HEAD_DIM)

    dtc, cum = {}, {}
    for c, bb in insts:
        dtc[c, bb] = dt_ref[bb, pl.ds(c * L, L), :]
        acc = jnp.zeros((L, LANES), F32)
        for part in _split3(dtc[c, bb] * a_neg):
            acc = acc + _dot(tril, part)
        cum[c, bb] = acc
    cum_rows = {i: pair_rows(cum[i]) for i in insts}

    b_g, c_g, cb2 = {}, {}, {}
    for c, bb in insts:
        for g in range(SSD_GROUPS):
            b_lo = SSD_WIDTH + g * D_STATE
            c_lo = SSD_WIDTH + (SSD_GROUPS + g) * D_STATE
            b_g[c, bb, g] = act_ref[bb, pl.ds(c * L, L), b_lo:b_lo + D_STATE].astype(BF16)
            c_g[c, bb, g] = act_ref[bb, pl.ds(c * L, L), c_lo:c_lo + D_STATE].astype(BF16)
            cb2[c, bb, g] = lax.dot_general(c_g[c, bb, g], jnp.concatenate([b_g[c, bb, g]] * 2, axis=0), _NT,
                                            preferred_element_type=F32)

    y_in, upd, exp_ccol, dec2 = {}, {}, {}, {}
    for c, bb in insts:
        last = cum[c, bb][L - 1:L, :]
        exp_last = jnp.exp(last)
        for k in pairs:
            g = k // pairs_per_group
            ccol = pair_cols(cum[c, bb], k)
            dtcol = pair_cols(dtc[c, bb], k)
            crow = cum_rows[c, bb][2 * k:2 * k + 1, :]
            decay = jnp.exp(jnp.where(tril2, ccol - crow, -jnp.inf))
            w2 = (decay * cb2[c, bb, g]).astype(BF16)
            xdt = act_ref[bb, pl.ds(c * L, L), pair_slice(k)] * dtcol
            xbd = jnp.concatenate([jnp.where(left, xdt, 0.0), jnp.where(left, 0.0, xdt)], axis=0).astype(BF16)
            y_in[c, bb, k] = _dot(w2, xbd)
            last2 = jnp.where(left[0:1, :], last[:, 2 * k:2 * k + 1], last[:, 2 * k + 1:2 * k + 2])
            xw = (xdt * jnp.exp(last2 - ccol)).astype(BF16)
            upd[c, bb, k] = lax.dot_general(xw, b_g[c, bb, g], _TN, preferred_element_type=F32)
            exp_ccol[c, bb, k] = jnp.exp(ccol)
            dec2[c, bb, k] = jnp.concatenate(
                [jnp.broadcast_to(exp_last[:, 2 * k:2 * k + 1], (SSD_HEAD_DIM, D_STATE)),
                 jnp.broadcast_to(exp_last[:, 2 * k + 1:2 * k + 2], (SSD_HEAD_DIM, D_STATE))], axis=0)

    st = {(bb, k): st_ref[bb, pair_slice(k), :] for bb in range(nb) for k in pairs}
    y_st = {}
    for c, bb in insts:
        for k in pairs:
            g = k // pairs_per_group
            y_st[c, bb, k] = lax.dot_general(c_g[c, bb, g], st[bb, k].astype(BF16), _NT,
                                             preferred_element_type=F32)
            st[bb, k] = dec2[c, bb, k] * st[bb, k] + upd[c, bb, k]
    for bb in range(nb):
        for k in pairs:
            st_ref[bb, pair_slice(k), :] = st[bb, k]

    for c, bb in insts:
        xs = act_ref[bb, pl.ds(c * L, L), 0:SSD_WIDTH]
        y = jnp.concatenate([y_in[c, bb, k] + exp_ccol[c, bb, k] * y_st[c, bb, k] for k in pairs], axis=1)
        y = (y + dskip_ref[...] * xs) * _silu(z_ref[bb, pl.ds(c * L, L), :])
        y_ref[bb, pl.ds(c * L, L), :] = _rms(y, nw_ref[...]).astype(BF16)

    @pl.when(j == pl.num_programs(1) - 1)
    def _():
        hout_ref[...] = st_ref[...]


def _ssd(xbc, z, dt, h0, conv0, B, T, lp):
    ts = min(SSD_ROWS_PER_STEP, T)
    nb = min(B, SSD_ROWS_PER_STEP // ts)
    assert T % ts == 0 and ts % CHUNK == 0 and B % nb == 0 and T >= CONV_W - 1
    nt = T // ts
    row = lambda b, j: (b, j, 0)
    const = lambda b, j: (0, 0)
    per_b = lambda b, j: (b, 0, 0)
    state_rows = SSD_HEADS * SSD_HEAD_DIM
    return pl.pallas_call(
        functools.partial(_ssd_kernel, ts=ts, nb=nb),
        grid=(B // nb, nt),
        in_specs=[
            pl.BlockSpec((nb, ts, CONV_DIM), row),
            pl.BlockSpec((nb, ts, SSD_WIDTH), row),
            pl.BlockSpec((nb, ts, LANES), row),
            pl.BlockSpec((nb, state_rows, D_STATE), per_b),
            pl.BlockSpec((nb, CONV_W - 1, CONV_DIM), per_b),
            pl.BlockSpec((CONV_W, CONV_DIM), const),
            pl.BlockSpec((1, CONV_DIM), const),
            pl.BlockSpec((1, LANES), const),
            pl.BlockSpec((1, SSD_WIDTH), const),
            pl.BlockSpec((1, SSD_WIDTH), const),
        ],
        out_specs=[
            pl.BlockSpec((nb, ts, SSD_WIDTH), row),
            pl.BlockSpec((nb, state_rows, D_STATE), per_b),
            pl.BlockSpec((nb, CONV_W - 1, CONV_DIM), per_b),
        ],
        out_shape=[
            jax.ShapeDtypeStruct((B, T, SSD_WIDTH), BF16),
            jax.ShapeDtypeStruct((B, state_rows, D_STATE), F32),
            jax.ShapeDtypeStruct((B, CONV_W - 1, CONV_DIM), F32),
        ],
        scratch_shapes=[
            pltpu.VMEM((nb, state_rows, D_STATE), F32),
            pltpu.VMEM((nb, ts + SUBLANES, CONV_DIM), F32),
            pltpu.VMEM((nb, ts, CONV_DIM), F32),
        ],
        compiler_params=pltpu.CompilerParams(
            dimension_semantics=("arbitrary", "arbitrary"), vmem_limit_bytes=VMEM_LIMIT),
        name="ssd",
    )(xbc.reshape(B, T, CONV_DIM), z.reshape(B, T, SSD_WIDTH), dt.reshape(B, T, LANES), h0, conv0,
      lp["conv_w"], lp["conv_b"], lp["a_log"], lp["d_skip"], lp["ssd_norm"])


def _attn_kernel(cqn_ref, keys_ref, wq_ref, wuk_ref, wuv_ref, cq_ref, sq_ref, o_ref,
                 q_scr, v_scr, s_scr, m_scr, acc_scr, *, tq, tk, past, n_masked):
    i = pl.program_id(1)
    rows = MLA_HEADS * tq
    lane_tiles = tk // LANES

    @pl.when(i == 0)
    def _():
        kv = keys_ref[...]
        lane = lax.broadcasted_iota(jnp.int32, kv.shape, 1)
        v_scr[...] = jnp.where(lane < KV_LORA, kv, jnp.ones_like(kv))

    qall = _dot(cqn_ref[...], wq_ref[...])
    for h in range(MLA_HEADS):
        t = qall[:, h * LANES:(h + 1) * LANES]
        r = t * cq_ref[...] + pltpu.roll(t, LANES - QK_ROPE, 1) * sq_ref[...]
        q_scr[h * tq:(h + 1) * tq, :] = _dot(r.astype(BF16), wuk_ref[h]).astype(BF16)
    m_scr[...] = jnp.full(m_scr.shape, -jnp.inf, F32)

    q_first = past + i * tq
    n_full = q_first // tk

    def for_each_run(first, count, fn):
        quads = count // 4

        def quad(t, carry):
            fn(first + 4 * t, 4)
            return carry

        lax.fori_loop(0, quads, quad, 0)
        start = first + 4 * quads
        for width in (2, 1):
            take = (count & width) != 0

            @pl.when(take)
            def _():
                fn(start, width)

            start = start + jnp.where(take, width, 0)

    def scores(jt, width, masked=False):
        k0 = pl.multiple_of(jt * tk, tk)
        s = lax.dot_general(q_scr[...], keys_ref[pl.ds(k0, width * tk), :], _NT,
                            preferred_element_type=F32)
        if masked:
            qc = (q_first + lax.broadcasted_iota(jnp.int32, (tq, width * tk), 0)) // CHUNK
            kc = (k0 + lax.broadcasted_iota(jnp.int32, (tq, width * tk), 1)) // CHUNK
            bias = jnp.where(kc <= qc, 0.0, -jnp.inf).astype(F32)
            s = (s.reshape(MLA_HEADS, tq, width * tk) + bias[None]).reshape(rows, width * tk)
        m = m_scr[...]
        for w in range(width):
            sw = s[:, w * tk:(w + 1) * tk]
            s_scr[jt + w] = sw
            for c in range(lane_tiles):
                m = jnp.maximum(m, sw[:, c * LANES:(c + 1) * LANES])
            if tk % LANES:
                m = jnp.maximum(m, jnp.max(sw[:, lane_tiles * LANES:], axis=-1, keepdims=True))
        m_scr[...] = m

    scores(n_full, n_masked, masked=True)
    for_each_run(0, n_full, scores)

    m_row = jnp.max(m_scr[...], axis=-1, keepdims=True)
    m_scr[...] = jnp.broadcast_to(m_row, m_scr.shape)

    def values(jt, width, first=False):
        k0 = pl.multiple_of(jt * tk, tk)
        mb = m_scr[...]
        if tk % LANES:
            mb = mb[:, 0:1]
        else:
            mb = jnp.concatenate([mb] * lane_tiles, axis=1)
        p = jnp.concatenate([jnp.exp2(s_scr[jt + w] - mb).astype(BF16) for w in range(width)], axis=1)
        pv = _dot(p, v_scr[pl.ds(k0, width * tk), :])
        if first:
            acc_scr[...] = pv
        else:
            acc_scr[...] += pv

    values(n_full, n_masked, first=True)
    for_each_run(0, n_full, values)

    for k in range(MLA_HEADS // 2):
        acc = acc_scr[2 * k * tq:(2 * k + 2) * tq, :]
        o_lat = acc[:, :KV_LORA] / acc[:, KV_LORA:]
        pair = jnp.concatenate([o_lat[:tq, :], o_lat[tq:, :]], axis=1).astype(BF16)
        o_ref[:, k * LANES:(k + 1) * LANES] = _dot(pair, wuv_ref[k]).astype(BF16)


def _attn(cqn, keys3d, B, T, past, lp, tabs):
    S = keys3d.shape[1]
    assert S == past + T
    if T >= 256:
        tq = tk = 256
        assert T % tq == 0 and past % tk == 0
        n_masked = tq // tk
    else:
        tq, tk = T, S
        n_masked = 1
    nq = T // tq
    rows = MLA_HEADS * tq
    const2 = lambda b, i: (0, 0)
    const3 = lambda b, i: (0, 0, 0)
    return pl.pallas_call(
        functools.partial(_attn_kernel, tq=tq, tk=tk, past=past, n_masked=n_masked),
        grid=(B, nq),
        in_specs=[
            pl.BlockSpec((tq, Q_LORA), lambda b, i: (b * nq + i, 0)),
            pl.BlockSpec((None, S, KEY_WIDTH), lambda b, i: (b, 0, 0)),
            pl.BlockSpec((Q_LORA, MLA_HEADS * LANES), const2),
            pl.BlockSpec((MLA_HEADS, LANES, KEY_WIDTH), const3),
            pl.BlockSpec((MLA_HEADS // 2, 2 * KV_LORA, LANES), const3),
            pl.BlockSpec((tq, LANES), lambda b, i: (i, 0)),
            pl.BlockSpec((tq, LANES), lambda b, i: (i, 0)),
        ],
        out_specs=pl.BlockSpec((tq, MLA_WIDTH), lambda b, i: (b * nq + i, 0)),
        out_shape=jax.ShapeDtypeStruct((B * T, MLA_WIDTH), BF16),
        scratch_shapes=[
            pltpu.VMEM((rows, KEY_WIDTH), BF16),
            pltpu.VMEM((S, KEY_WIDTH), BF16),
            pltpu.VMEM((S // tk, rows, tk), F32),
            pltpu.VMEM((rows, LANES), F32),
            pltpu.VMEM((rows, KEY_WIDTH), F32),
        ],
        compiler_params=pltpu.CompilerParams(
            dimension_semantics=("arbitrary", "arbitrary"), vmem_limit_bytes=VMEM_LIMIT),
        name="attn",
    )(cqn, keys3d, lp["w_q"], lp["w_uk"], lp["w_uv"], tabs["cq"], tabs["sq"])


def _ffn_kernel(x_ref, y_ref, o_ref, woa_ref, wob_ref, nf_ref, wg_ref, wu_ref, wd_ref, nfin_ref,
                xo_ref, *, final):
    tm = x_ref.shape[0]
    sub = min(tm, FFN_SUBTILE)
    for r0 in range(0, tm, sub):
        rows = slice(r0, r0 + sub)
        x1 = x_ref[rows, :] + _dot(y_ref[rows, :], woa_ref[...]) + _dot(o_ref[rows, :], wob_ref[...])
        hf = _rms(x1, nf_ref[...]).astype(BF16)
        g = _dot(hf, wg_ref[...])
        u = _dot(hf, wu_ref[...])
        x2 = x1 + _dot((_silu(g) * u).astype(BF16), wd_ref[...])
        if final:
            x2 = _rms(x2, nfin_ref[...])
        xo_ref[rows, :] = x2


def _ffn(x2d, y, o, lp, norm_final, final):
    n = x2d.shape[0]
    tm = min(512, n)
    assert n % tm == 0
    row = lambda i: (i, 0)
    const = lambda i: (0, 0)
    resident = functools.partial(pl.BlockSpec, index_map=const, pipeline_mode=pl.Buffered(1))
    return pl.pallas_call(
        functools.partial(_ffn_kernel, final=final),
        grid=(n // tm,),
        in_specs=[
            pl.BlockSpec((tm, D_MODEL), row),
            pl.BlockSpec((tm, SSD_WIDTH), row),
            pl.BlockSpec((tm, MLA_WIDTH), row),
            resident((SSD_WIDTH, D_MODEL)),
            resident((MLA_WIDTH, D_MODEL)),
            pl.BlockSpec((1, D_MODEL), const),
            resident((D_MODEL, D_FF)),
            resident((D_MODEL, D_FF)),
            resident((D_FF, D_MODEL)),
            pl.BlockSpec((1, D_MODEL), const),
        ],
        out_specs=pl.BlockSpec((tm, D_MODEL), row),
        out_shape=jax.ShapeDtypeStruct((n, D_MODEL), F32),
        compiler_params=pltpu.CompilerParams(
            dimension_semantics=("arbitrary",), vmem_limit_bytes=VMEM_LIMIT),
        name="ffn",
    )(x2d, y, o, lp["w_out_a"], lp["w_out_b"], lp["norm_ffn"], lp["w_gate"], lp["w_up"], lp["w_down"], norm_final)


def _swap_halves_cols(w):
    half = w.shape[-1] // 2
    return jnp.concatenate([w[..., half:], w[..., :half]], axis=-1)


def _layer_params(l, w_in, w_uq, w_uk, w_uv, w_out, norm_mix, q_norm, kv_norm, ssd_norm,
                  conv_w, conv_b, dt_bias, a_log, d_skip, norm_ffn, w_gate, w_up, w_down):
    wi = w_in[l]
    w_kr = wi[:, S_Q + KV_LORA:]
    misc = jnp.concatenate([
        wi[:, S_XBC:S_DT], jnp.zeros((D_MODEL, MISC_KR - SSD_HEADS), F32),
        w_kr, _swap_halves_cols(w_kr),
        jnp.zeros((D_MODEL, LANES - MISC_KR_SWAPPED - QK_ROPE), F32)], axis=1)
    w_in_r = jnp.concatenate([wi[:, :S_XBC], wi[:, S_DT:S_Q], wi[:, S_Q:S_Q + KV_LORA], misc], axis=1)

    wq = w_uq[l].reshape(Q_LORA, MLA_HEADS, QK_NOPE + QK_ROPE)
    wq_rope = wq[:, :, QK_NOPE:]
    w_q = jnp.concatenate([wq, _swap_halves_cols(wq_rope)], axis=-1).reshape(Q_LORA, MLA_HEADS * LANES)

    uk_t = jnp.transpose(w_uk[l], (1, 2, 0))
    eye = jnp.eye(QK_ROPE, dtype=F32)
    top = jnp.concatenate([uk_t, jnp.zeros((MLA_HEADS, QK_NOPE, KEY_WIDTH - KV_LORA), F32)], axis=2)
    mid = jnp.concatenate([jnp.zeros((QK_ROPE, KV_LORA), F32), eye,
                           jnp.zeros((QK_ROPE, KEY_WIDTH - KV_LORA - QK_ROPE), F32)], axis=1)
    mid = jnp.broadcast_to(mid[None], (MLA_HEADS, QK_ROPE, KEY_WIDTH))
    bot = jnp.zeros((MLA_HEADS, LANES - QK_NOPE - QK_ROPE, KEY_WIDTH), F32)
    w_uk_p = jnp.concatenate([top, mid, bot], axis=1)

    uv = jnp.transpose(w_uv[l], (1, 0, 2)).reshape(MLA_HEADS // 2, 2, KV_LORA, V_HEAD)
    zero = jnp.zeros((MLA_HEADS // 2, KV_LORA, V_HEAD), F32)
    w_uv_p = jnp.concatenate([jnp.concatenate([uv[:, 0], zero], axis=2),
                              jnp.concatenate([zero, uv[:, 1]], axis=2)], axis=1)

    pad_heads = lambda v: jnp.concatenate([v, jnp.zeros((LANES - SSD_HEADS,), F32)])[None]
    return {
        "w_in": w_in_r.astype(BF16), "w_q": w_q.astype(BF16), "w_uk": w_uk_p.astype(BF16),
        "w_uv": w_uv_p.astype(BF16),
        "w_out_a": w_out[l][:SSD_WIDTH].astype(BF16), "w_out_b": w_out[l][SSD_WIDTH:].astype(BF16),
        "w_gate": w_gate[l].astype(BF16), "w_up": w_up[l].astype(BF16), "w_down": w_down[l].astype(BF16),
        "norm_mix": norm_mix[l][None], "q_norm": q_norm[l][None], "kv_norm": kv_norm[l][None],
        "ssd_norm": ssd_norm[l][None], "norm_ffn": norm_ffn[l][None],
        "conv_w": conv_w[l], "conv_b": conv_b[l][None],
        "dt_bias": pad_heads(dt_bias[l]), "a_log": pad_heads(a_log[l]),
        "d_skip": jnp.repeat(d_skip[l], SSD_HEAD_DIM)[None],
    }


def _rope_tables(past, T):
    pos = (past + jnp.arange(T)).astype(F32)
    inv = 1.0 / (ROPE_BASE ** (jnp.arange(0, QK_ROPE, 2, dtype=F32) / QK_ROPE))
    ang = pos[:, None] * inv[None, :]
    cos, sin = jnp.cos(ang), jnp.sin(ang)
    c32 = jnp.concatenate([cos, cos], axis=1)
    s32 = jnp.concatenate([-sin, sin], axis=1)
    scale = (QK_NOPE + QK_ROPE) ** -0.5 * LOG2_E
    zeros = lambda w: jnp.zeros((T, w), F32)
    return {
        "ck": jnp.concatenate([c32, zeros(LANES - QK_ROPE)], axis=1),
        "sk": jnp.concatenate([s32, zeros(LANES - QK_ROPE)], axis=1),
        "cq": scale * jnp.concatenate([jnp.ones((T, QK_NOPE), F32), c32, zeros(LANES - QK_NOPE - QK_ROPE)], axis=1),
        "sq": scale * jnp.concatenate([zeros(QK_NOPE), s32, zeros(LANES - QK_NOPE - QK_ROPE)], axis=1),
    }


def _hybrid_layer(x2d, B, T, past_keys, h0, conv0, lp, tabs, norm_final, final):
    past = 0 if past_keys is None else past_keys.shape[1]
    z, xbc, cqn, ckv, kr, keys, dt = _in_proj(x2d, T, lp, tabs)
    y, h_new, conv_new = _ssd(xbc, z, dt, h0, conv0, B, T, lp)
    y = y.reshape(B * T, SSD_WIDTH)
    keys3d = keys.reshape(B, T, KEY_WIDTH)
    if past_keys is not None:
        keys3d = jnp.concatenate([past_keys, keys3d], axis=1)
    o = _attn(cqn, keys3d, B, T, past, lp, tabs)
    x_new = _ffn(x2d, y, o, lp, norm_final, final)
    return x_new, ckv, kr, h_new, conv_new


def kernel(x_prompt, x_sample, cache_mla_ckv, cache_mla_krope, state_ssm, state_conv, w_in, w_uq, w_uk, w_uv, w_out, norm_mix, q_norm, kv_norm, ssd_norm, conv_w, conv_b, dt_bias, a_log, d_skip, norm_ffn, w_gate, w_up, w_down, norm_final):
    depth = w_in.shape[0]
    bp, tp, _ = x_prompt.shape
    bs, tsm, _ = x_sample.shape
    past = cache_mla_ckv.shape[2]
    state_rows = SSD_HEADS * SSD_HEAD_DIM

    xp = x_prompt.reshape(bp * tp, D_MODEL)
    xs = x_sample.reshape(bs * tsm, D_MODEL)
    tabs_p = _rope_tables(0, tp)
    tabs_s = _rope_tables(past, tsm)
    zero_state = jnp.zeros((bp, state_rows, D_STATE), F32)
    zero_conv = jnp.zeros((bp, CONV_W - 1, CONV_DIM), F32)
    nfin = norm_final[None]

    outs_p = [[], [], [], []]
    outs_s = [[], [], [], []]
    for l in range(depth):
        lp = _layer_params(l, w_in, w_uq, w_uk, w_uv, w_out, norm_mix, q_norm, kv_norm, ssd_norm,
                           conv_w, conv_b, dt_bias, a_log, d_skip, norm_ffn, w_gate, w_up, w_down)
        final = l == depth - 1
        xp, ckv, kr, hs, cb = _hybrid_layer(xp, bp, tp, None, zero_state, zero_conv, lp, tabs_p, nfin, final)
        for acc, v in zip(outs_p, (ckv.reshape(bp, tp, KV_LORA), kr.reshape(bp, tp, QK_ROPE),
                                   hs.reshape(bp, SSD_HEADS, SSD_HEAD_DIM, D_STATE), cb)):
            acc.append(v)
        past_keys = jnp.concatenate(
            [cache_mla_ckv[l], cache_mla_krope[l],
             jnp.zeros((bs, past, KEY_WIDTH - KV_LORA - QK_ROPE), F32)], axis=-1).astype(BF16)
        xs, ckv, kr, hs, cb = _hybrid_layer(
            xs, bs, tsm, past_keys, state_ssm[l].reshape(bs, state_rows, D_STATE), state_conv[l],
            lp, tabs_s, nfin, final)
        for acc, v in zip(outs_s, (ckv.reshape(bs, tsm, KV_LORA), kr.reshape(bs, tsm, QK_ROPE),
                                   hs.reshape(bs, SSD_HEADS, SSD_HEAD_DIM, D_STATE), cb)):
            acc.append(v)
    return (xp.reshape(bp, tp, D_MODEL), xs.reshape(bs, tsm, D_MODEL),
            *(jnp.stack(v) for v in outs_p), *(jnp.stack(v) for v in outs_s))
```

```python
import functools

import jax
import jax.numpy as jnp
from jax import lax
from jax.experimental import pallas as pl
from jax.experimental.pallas import tpu as pltpu

F32 = jnp.float32
BF16 = jnp.bfloat16

D_MODEL = 1024
CHUNK = 64
EPS = 1e-6
SSD_HEADS = 8
SSD_HEAD_DIM = 64
SSD_WIDTH = SSD_HEADS * SSD_HEAD_DIM
SSD_GROUPS = 2
D_STATE = 128
CONV_W = 4
CONV_DIM = SSD_WIDTH + 2 * SSD_GROUPS * D_STATE
MLA_HEADS = 8
QK_NOPE = 64
QK_ROPE = 32
V_HEAD = 64
Q_LORA = 256
KV_LORA = 128
MLA_WIDTH = MLA_HEADS * V_HEAD
ROPE_BASE = 10000.0
LOG2_E = 1.4426950408889634
S_Z = SSD_WIDTH
S_XBC = S_Z + CONV_DIM
S_DT = S_XBC + SSD_HEADS
S_Q = S_DT + Q_LORA
D_FF = 2816

LANES = 128
SUBLANES = 8
KEY_WIDTH = 2 * LANES
PROJ_WIDTH = 2048
MISC_KR = 32
MISC_KR_SWAPPED = 64
VMEM_LIMIT = 56 * 1024 * 1024
IN_PROJ_SUBTILE = 128
FFN_SUBTILE = 256
SSD_ROWS_PER_STEP = 1024

_NT = (((1,), (1,)), ((), ()))
_TN = (((0,), (0,)), ((), ()))


def _dot(a, b):
    return jnp.dot(a, b, preferred_element_type=F32)


def _rms(x, w):
    return x * lax.rsqrt(jnp.mean(x * x, axis=-1, keepdims=True) + EPS) * w


def _silu(x):
    return x * jax.nn.sigmoid(x)


def _split3(x):
    hi = x.astype(BF16)
    r = x - hi.astype(F32)
    mid = r.astype(BF16)
    lo = (r - mid.astype(F32)).astype(BF16)
    return hi, mid, lo


def _in_proj_kernel(x_ref, nw_ref, w_ref, qn_ref, kvn_ref, dtb_ref, ck_ref, sk_ref,
                    z_ref, xbc_ref, cqn_ref, ckv_ref, kr_ref, keys_ref, dt_ref):
    tm = x_ref.shape[0]
    sub = min(tm, IN_PROJ_SUBTILE)
    for r0 in range(0, tm, sub):
        rows = slice(r0, r0 + sub)
        h = _rms(x_ref[rows, :], nw_ref[...]).astype(BF16)
        z_ref[rows, :] = _dot(h, w_ref[:, 0:512])
        xbc_ref[rows, :] = _dot(h, w_ref[:, 512:1536])
        cqn_ref[rows, :] = _rms(_dot(h, w_ref[:, 1536:1792]), qn_ref[...]).astype(BF16)
        tail = _dot(h, w_ref[:, 1792:2048])
        ckv = _rms(tail[:, :LANES], kvn_ref[...])
        ckv_ref[rows, :] = ckv
        misc = tail[:, LANES:]
        kr = (pltpu.roll(misc, LANES - MISC_KR, 1) * ck_ref[rows, :]
              + pltpu.roll(misc, LANES - MISC_KR_SWAPPED, 1) * sk_ref[rows, :])
        kr_ref[rows, :] = kr[:, :QK_ROPE]
        keys_ref[rows, :] = jnp.concatenate([ckv, kr], axis=1).astype(BF16)
        lane = lax.broadcasted_iota(jnp.int32, misc.shape, 1)
        dt_ref[rows, :] = jnp.where(lane < SSD_HEADS, jax.nn.softplus(misc + dtb_ref[...]), 0.0)


def _in_proj(x2d, T, lp, tabs):
    n = x2d.shape[0]
    tm = min(1024, n)
    assert n % tm == 0
    if tm <= T:
        assert T % tm == 0
        per_seq = T // tm
        ck, sk = tabs["ck"], tabs["sk"]
        tab_map = lambda i: (i % per_seq, 0)
    else:
        assert tm % T == 0
        ck = jnp.tile(tabs["ck"], (tm // T, 1))
        sk = jnp.tile(tabs["sk"], (tm // T, 1))
        tab_map = lambda i: (0, 0)
    row = lambda i: (i, 0)
    const = lambda i: (0, 0)
    return pl.pallas_call(
        _in_proj_kernel,
        grid=(n // tm,),
        in_specs=[
            pl.BlockSpec((tm, D_MODEL), row),
            pl.BlockSpec((1, D_MODEL), const),
            pl.BlockSpec((D_MODEL, PROJ_WIDTH), const),
            pl.BlockSpec((1, Q_LORA), const),
            pl.BlockSpec((1, KV_LORA), const),
            pl.BlockSpec((1, LANES), const),
            pl.BlockSpec((tm, LANES), tab_map),
            pl.BlockSpec((tm, LANES), tab_map),
        ],
        out_specs=[
            pl.BlockSpec((tm, SSD_WIDTH), row),
            pl.BlockSpec((tm, CONV_DIM), row),
            pl.BlockSpec((tm, Q_LORA), row),
            pl.BlockSpec((tm, KV_LORA), row),
            pl.BlockSpec((tm, QK_ROPE), row),
            pl.BlockSpec((tm, KEY_WIDTH), row),
            pl.BlockSpec((tm, LANES), row),
        ],
        out_shape=[
            jax.ShapeDtypeStruct((n, SSD_WIDTH), F32),
            jax.ShapeDtypeStruct((n, CONV_DIM), F32),
            jax.ShapeDtypeStruct((n, Q_LORA), BF16),
            jax.ShapeDtypeStruct((n, KV_LORA), F32),
            jax.ShapeDtypeStruct((n, QK_ROPE), F32),
            jax.ShapeDtypeStruct((n, KEY_WIDTH), BF16),
            jax.ShapeDtypeStruct((n, LANES), F32),
        ],
        compiler_params=pltpu.CompilerParams(
            dimension_semantics=("arbitrary",), vmem_limit_bytes=VMEM_LIMIT),
        name="in_proj",
    )(x2d, lp["norm_mix"], lp["w_in"], lp["q_norm"], lp["kv_norm"], lp["dt_bias"], ck, sk)


def _ssd_kernel(xbc_ref, z_ref, dt_ref, h0_ref, cb0_ref, cw_ref, cbias_ref, alog_ref, dskip_ref, nw_ref,
                y_ref, hout_ref, cout_ref, st_ref, ubuf_ref, act_ref, *, ts, nb):
    j = pl.program_id(1)
    pad = SUBLANES

    @pl.when(j == 0)
    def _():
        st_ref[...] = h0_ref[...]
        ubuf_ref[:, 0:pad, :] = jnp.zeros((nb, pad, CONV_DIM), F32)
        ubuf_ref[:, pad - (CONV_W - 1):pad, :] = cb0_ref[...]

    for bb in range(nb):
        ubuf_ref[bb, pad:pad + ts, :] = xbc_ref[bb]
        cur = ubuf_ref[bb, pad:pad + ts, :].reshape(ts // SUBLANES, SUBLANES, CONV_DIM)
        above = ubuf_ref[bb, 0:ts, :].reshape(ts // SUBLANES, SUBLANES, CONV_DIM)
        sub3 = lax.broadcasted_iota(jnp.int32, cur.shape, 1)
        yc = cbias_ref[...].reshape(1, 1, CONV_DIM)
        for k in range(CONV_W):
            d = CONV_W - 1 - k
            u = cur if d == 0 else pltpu.roll(jnp.where(sub3 >= SUBLANES - d, above, cur), d, 1)
            yc = yc + u * cw_ref[k:k + 1, :].reshape(1, 1, CONV_DIM)
        act_ref[bb] = _silu(yc).reshape(ts, CONV_DIM)
        cout_ref[bb] = ubuf_ref[bb, pad + ts - (CONV_W - 1):pad + ts, :]
        ubuf_ref[bb, 0:pad, :] = ubuf_ref[bb, ts:ts + pad, :]

    L = CHUNK
    lane = lax.broadcasted_iota(jnp.int32, (L, LANES), 1)
    sub = lax.broadcasted_iota(jnp.int32, (L, LANES), 0)
    left = lane < L
    tril2 = (lane % L) <= sub
    r64 = lax.broadcasted_iota(jnp.int32, (L, L), 0)
    c64 = lax.broadcasted_iota(jnp.int32, (L, L), 1)
    tril = (c64 <= r64).astype(BF16)
    sel8 = (lax.broadcasted_iota(jnp.int32, (8, LANES), 0)
            == lax.broadcasted_iota(jnp.int32, (8, LANES), 1)).astype(BF16)
    head_lane = lax.broadcasted_iota(jnp.int32, (1, LANES), 1) < SSD_HEADS
    a_neg = jnp.where(head_lane, -jnp.exp(alog_ref[...]), 0.0)

    def pair_rows(v):
        stacked = jnp.concatenate([v, pltpu.roll(v, LANES - 1, 1)], axis=0)
        out = jnp.zeros((8, LANES), F32)
        for part in _split3(stacked):
            out = out + lax.dot_general(sel8, part, _NT, preferred_element_type=F32)
        return out

    def pair_cols(v, k):
        return jnp.where(left, v[:, 2 * k:2 * k + 1], v[:, 2 * k + 1:2 * k + 2])

    pairs_per_group = SSD_HEADS // SSD_GROUPS // 2
    pairs = range(SSD_HEADS // 2)
    insts = [(c, bb) for c in range(ts // L) for bb in range(nb)]

    def pair_slice(k):
        return slice(2 * k * SSD_HEAD_DIM, (2 * k + 2) * SSD_HEAD_DIM)

    dtc, cum = {}, {}
    for c, bb in insts:
        dtc[c, bb] = dt_ref[bb, pl.ds(c * L, L), :]
        acc = jnp.zeros((L, LANES), F32)
        for part in _split3(dtc[c, bb] * a_neg):
            acc = acc + _dot(tril, part)
        cum[c, bb] = acc
    cum_rows = {i: pair_rows(cum[i]) for i in insts}

    b_g, c_g, cb2 = {}, {}, {}
    for c, bb in insts:
        for g in range(SSD_GROUPS):
            b_lo = SSD_WIDTH + g * D_STATE
            c_lo = SSD_WIDTH + (SSD_GROUPS + g) * D_STATE
            b_g[c, bb, g] = act_ref[bb, pl.ds(c * L, L), b_lo:b_lo + D_STATE].astype(BF16)
            c_g[c, bb, g] = act_ref[bb, pl.ds(c * L, L), c_lo:c_lo + D_STATE].astype(BF16)
            cb2[c, bb, g] = lax.dot_general(c_g[c, bb, g], jnp.concatenate([b_g[c, bb, g]] * 2, axis=0), _NT,
                                            preferred_element_type=F32)

    y_in, upd, exp_ccol, dec2 = {}, {}, {}, {}
    for c, bb in insts:
        last = cum[c, bb][L - 1:L, :]
        exp_last = jnp.exp(last)
        for k in pairs:
            g = k // pairs_per_group
            ccol = pair_cols(cum[c, bb], k)
            dtcol = pair_cols(dtc[c, bb], k)
            crow = cum_rows[c, bb][2 * k:2 * k + 1, :]
            decay = jnp.exp(jnp.where(tril2, ccol - crow, -jnp.inf))
            w2 = (decay * cb2[c, bb, g]).astype(BF16)
            xdt = act_ref[bb, pl.ds(c * L, L), pair_slice(k)] * dtcol
            xbd = jnp.concatenate([jnp.where(left, xdt, 0.0), jnp.where(left, 0.0, xdt)], axis=0).astype(BF16)
            y_in[c, bb, k] = _dot(w2, xbd)
            last2 = jnp.where(left[0:1, :], last[:, 2 * k:2 * k + 1], last[:, 2 * k + 1:2 * k + 2])
            xw = (xdt * jnp.exp(last2 - ccol)).astype(BF16)
            upd[c, bb, k] = lax.dot_general(xw, b_g[c, bb, g], _TN, preferred_element_type=F32)
            exp_ccol[c, bb, k] = jnp.exp(ccol)
            dec2[c, bb, k] = jnp.concatenate(
                [jnp.broadcast_to(exp_last[:, 2 * k:2 * k + 1], (SSD_HEAD_DIM, D_STATE)),
                 jnp.broadcast_to(exp_last[:, 2 * k + 1:2 * k + 2], (SSD_HEAD_DIM, D_STATE))], axis=0)

    st = {(bb, k): st_ref[bb, pair_slice(k), :] for bb in range(nb) for k in pairs}
    y_st = {}
    for c, bb in insts:
        for k in pairs:
            g = k // pairs_per_group
            y_st[c, bb, k] = lax.dot_general(c_g[c, bb, g], st[bb, k].astype(BF16), _NT,
                                             preferred_element_type=F32)
            st[bb, k] = dec2[c, bb, k] * st[bb, k] + upd[c, bb, k]
    for bb in range(nb):
        for k in pairs:
            st_ref[bb, pair_slice(k), :] = st[bb, k]

    for c, bb in insts:
        xs = act_ref[bb, pl.ds(c * L, L), 0:SSD_WIDTH]
        y = jnp.concatenate([y_in[c, bb, k] + exp_ccol[c, bb, k] * y_st[c, bb, k] for k in pairs], axis=1)
        y = (y + dskip_ref[...] * xs) * _silu(z_ref[bb, pl.ds(c * L, L), :])
        y_ref[bb, pl.ds(c * L, L), :] = _rms(y, nw_ref[...]).astype(BF16)

    @pl.when(j == pl.num_programs(1) - 1)
    def _():
        hout_ref[...] = st_ref[...]


def _ssd(xbc, z, dt, h0, conv0, B, T, lp):
    ts = min(SSD_ROWS_PER_STEP, T)
    nb = min(B, SSD_ROWS_PER_STEP // ts)
    assert T % ts == 0 and ts % CHUNK == 0 and B % nb == 0 and T >= CONV_W - 1
    nt = T // ts
    row = lambda b, j: (b, j, 0)
    const = lambda b, j: (0, 0)
    per_b = lambda b, j: (b, 0, 0)
    state_rows = SSD_HEADS * SSD_HEAD_DIM
    return pl.pallas_call(
        functools.partial(_ssd_kernel, ts=ts, nb=nb),
        grid=(B // nb, nt),
        in_specs=[
            pl.BlockSpec((nb, ts, CONV_DIM), row),
            pl.BlockSpec((nb, ts, SSD_WIDTH), row),
            pl.BlockSpec((nb, ts, LANES), row),
            pl.BlockSpec((nb, state_rows, D_STATE), per_b),
            pl.BlockSpec((nb, CONV_W - 1, CONV_DIM), per_b),
            pl.BlockSpec((CONV_W, CONV_DIM), const),
            pl.BlockSpec((1, CONV_DIM), const),
            pl.BlockSpec((1, LANES), const),
            pl.BlockSpec((1, SSD_WIDTH), const),
            pl.BlockSpec((1, SSD_WIDTH), const),
        ],
        out_specs=[
            pl.BlockSpec((nb, ts, SSD_WIDTH), row),
            pl.BlockSpec((nb, state_rows, D_STATE), per_b),
            pl.BlockSpec((nb, CONV_W - 1, CONV_DIM), per_b),
        ],
        out_shape=[
            jax.ShapeDtypeStruct((B, T, SSD_WIDTH), BF16),
            jax.ShapeDtypeStruct((B, state_rows, D_STATE), F32),
            jax.ShapeDtypeStruct((B, CONV_W - 1, CONV_DIM), F32),
        ],
        scratch_shapes=[
            pltpu.VMEM((nb, state_rows, D_STATE), F32),
            pltpu.VMEM((nb, ts + SUBLANES, CONV_DIM), F32),
            pltpu.VMEM((nb, ts, CONV_DIM), F32),
        ],
        compiler_params=pltpu.CompilerParams(
            dimension_semantics=("arbitrary", "arbitrary"), vmem_limit_bytes=VMEM_LIMIT),
        name="ssd",
    )(xbc.reshape(B, T, CONV_DIM), z.reshape(B, T, SSD_WIDTH), dt.reshape(B, T, LANES), h0, conv0,
      lp["conv_w"], lp["conv_b"], lp["a_log"], lp["d_skip"], lp["ssd_norm"])


def _attn_kernel(cqn_ref, keys_ref, wq_ref, wuk_ref, wuv_ref, cq_ref, sq_ref, o_ref,
                 q_scr, v_scr, s_scr, m_scr, acc_scr, *, tq, tk, past, n_masked):
    i = pl.program_id(1)
    rows = MLA_HEADS * tq
    lane_tiles = tk // LANES

    @pl.when(i == 0)
    def _():
        kv = keys_ref[...]
        lane = lax.broadcasted_iota(jnp.int32, kv.shape, 1)
        v_scr[...] = jnp.where(lane < KV_LORA, kv, jnp.ones_like(kv))

    qall = _dot(cqn_ref[...], wq_ref[...])
    for h in range(MLA_HEADS):
        t = qall[:, h * LANES:(h + 1) * LANES]
        r = t * cq_ref[...] + pltpu.roll(t, LANES - QK_ROPE, 1) * sq_ref[...]
        q_scr[h * tq:(h + 1) * tq, :] = _dot(r.astype(BF16), wuk_ref[h]).astype(BF16)
    m_scr[...] = jnp.full(m_scr.shape, -jnp.inf, F32)

    q_first = past + i * tq
    n_full = q_first // tk

    def for_each_run(first, count, fn):
        quads = count // 4

        def quad(t, carry):
            fn(first + 4 * t, 4)
            return carry

        lax.fori_loop(0, quads, quad, 0)
        start = first + 4 * quads
        for width in (2, 1):
            take = (count & width) != 0

            @pl.when(take)
            def _():
                fn(start, width)

            start = start + jnp.where(take, width, 0)

    def scores(jt, width, masked=False):
        k0 = pl.multiple_of(jt * tk, tk)
        s = lax.dot_general(q_scr[...], keys_ref[pl.ds(k0, width * tk), :], _NT,
                            preferred_element_type=F32)
        if masked:
            qc = (q_first + lax.broadcasted_iota(jnp.int32, (tq, width * tk), 0)) // CHUNK
            kc = (k0 + lax.broadcasted_iota(jnp.int32, (tq, width * tk), 1)) // CHUNK
            bias = jnp.where(kc <= qc, 0.0, -jnp.inf).astype(F32)
            s = (s.reshape(MLA_HEADS, tq, width * tk) + bias[None]).reshape(rows, width * tk)
        m = m_scr[...]
        for w in range(width):
            sw = s[:, w * tk:(w + 1) * tk]
            s_scr[jt + w] = sw
            for c in range(lane_tiles):
                m = jnp.maximum(m, sw[:, c * LANES:(c + 1) * LANES])
            if tk % LANES:
                m = jnp.maximum(m, jnp.max(sw[:, lane_tiles * LANES:], axis=-1, keepdims=True))
        m_scr[...] = m

    scores(n_full, n_masked, masked=True)
    for_each_run(0, n_full, scores)

    m_row = jnp.max(m_scr[...], axis=-1, keepdims=True)
    m_scr[...] = jnp.broadcast_to(m_row, m_scr.shape)

    def values(jt, width, first=False):
        k0 = pl.multiple_of(jt * tk, tk)
        mb = m_scr[...]
        if tk % LANES:
            mb = mb[:, 0:1]
        else:
            mb = jnp.concatenate([mb] * lane_tiles, axis=1)
        p = jnp.concatenate([jnp.exp2(s_scr[jt + w] - mb).astype(BF16) for w in range(width)], axis=1)
        pv = _dot(p, v_scr[pl.ds(k0, width * tk), :])
        if first:
            acc_scr[...] = pv
        else:
            acc_scr[...] += pv

    values(n_full, n_masked, first=True)
    for_each_run(0, n_full, values)

    for k in range(MLA_HEADS // 2):
        acc = acc_scr[2 * k * tq:(2 * k + 2) * tq, :]
        o_lat = acc[:, :KV_LORA] / acc[:, KV_LORA:]
        pair = jnp.concatenate([o_lat[:tq, :], o_lat[tq:, :]], axis=1).astype(BF16)
        o_ref[:, k * LANES:(k + 1) * LANES] = _dot(pair, wuv_ref[k]).astype(BF16)


def _attn(cqn, keys3d, B, T, past, lp, tabs):
    S = keys3d.shape[1]
    assert S == past + T
    if T >= 256:
        tq = tk = 256
        assert T % tq == 0 and past % tk == 0
        n_masked = tq // tk
    else:
        tq, tk = T, S
        n_masked = 1
    nq = T // tq
    rows = MLA_HEADS * tq
    const2 = lambda b, i: (0, 0)
    const3 = lambda b, i: (0, 0, 0)
    return pl.pallas_call(
        functools.partial(_attn_kernel, tq=tq, tk=tk, past=past, n_masked=n_masked),
        grid=(B, nq),
        in_specs=[
            pl.BlockSpec((tq, Q_LORA), lambda b, i: (b * nq + i, 0)),
            pl.BlockSpec((None, S, KEY_WIDTH), lambda b, i: (b, 0, 0)),
            pl.BlockSpec((Q_LORA, MLA_HEADS * LANES), const2),
            pl.BlockSpec((MLA_HEADS, LANES, KEY_WIDTH), const3),
            pl.BlockSpec((MLA_HEADS // 2, 2 * KV_LORA, LANES), const3),
            pl.BlockSpec((tq, LANES), lambda b, i: (i, 0)),
            pl.BlockSpec((tq, LANES), lambda b, i: (i, 0)),
        ],
        out_specs=pl.BlockSpec((tq, MLA_WIDTH), lambda b, i: (b * nq + i, 0)),
        out_shape=jax.ShapeDtypeStruct((B * T, MLA_WIDTH), BF16),
        scratch_shapes=[
            pltpu.VMEM((rows, KEY_WIDTH), BF16),
            pltpu.VMEM((S, KEY_WIDTH), BF16),
            pltpu.VMEM((S // tk, rows, tk), F32),
            pltpu.VMEM((rows, LANES), F32),
            pltpu.VMEM((rows, KEY_WIDTH), F32),
        ],
        compiler_params=pltpu.CompilerParams(
            dimension_semantics=("arbitrary", "arbitrary"), vmem_limit_bytes=VMEM_LIMIT),
        name="attn",
    )(cqn, keys3d, lp["w_q"], lp["w_uk"], lp["w_uv"], tabs["cq"], tabs["sq"])


def _ffn_kernel(x_ref, y_ref, o_ref, woa_ref, wob_ref, nf_ref, wg_ref, wu_ref, wd_ref, nfin_ref,
                xo_ref, *, final):
    tm = x_ref.shape[0]
    sub = min(tm, FFN_SUBTILE)
    for r0 in range(0, tm, sub):
        rows = slice(r0, r0 + sub)
        x1 = x_ref[rows, :] + _dot(y_ref[rows, :], woa_ref[...]) + _dot(o_ref[rows, :], wob_ref[...])
        hf = _rms(x1, nf_ref[...]).astype(BF16)
        g = _dot(hf, wg_ref[...])
        u = _dot(hf, wu_ref[...])
        x2 = x1 + _dot((_silu(g) * u).astype(BF16), wd_ref[...])
        if final:
            x2 = _rms(x2, nfin_ref[...])
        xo_ref[rows, :] = x2


def _ffn(x2d, y, o, lp, norm_final, final):
    n = x2d.shape[0]
    tm = min(512, n)
    assert n % tm == 0
    row = lambda i: (i, 0)
    const = lambda i: (0, 0)
    resident = functools.partial(pl.BlockSpec, index_map=const, pipeline_mode=pl.Buffered(1))
    return pl.pallas_call(
        functools.partial(_ffn_kernel, final=final),
        grid=(n // tm,),
        in_specs=[
            pl.BlockSpec((tm, D_MODEL), row),
            pl.BlockSpec((tm, SSD_WIDTH), row),
            pl.BlockSpec((tm, MLA_WIDTH), row),
            resident((SSD_WIDTH, D_MODEL)),
            resident((MLA_WIDTH, D_MODEL)),
            pl.BlockSpec((1, D_MODEL), const),
            resident((D_MODEL, D_FF)),
            resident((D_MODEL, D_FF)),
            resident((D_FF, D_MODEL)),
            pl.BlockSpec((1, D_MODEL), const),
        ],
        out_specs=pl.BlockSpec((tm, D_MODEL), row),
        out_shape=jax.ShapeDtypeStruct((n, D_MODEL), F32),
        compiler_params=pltpu.CompilerParams(
            dimension_semantics=("arbitrary",), vmem_limit_bytes=VMEM_LIMIT),
        name="ffn",
    )(x2d, y, o, lp["w_out_a"], lp["w_out_b"], lp["norm_ffn"], lp["w_gate"], lp["w_up"], lp["w_down"], norm_final)


def _swap_halves_cols(w):
    half = w.shape[-1] // 2
    return jnp.concatenate([w[..., half:], w[..., :half]], axis=-1)


def _layer_params(l, w_in, w_uq, w_uk, w_uv, w_out, norm_mix, q_norm, kv_norm, ssd_norm,
                  conv_w, conv_b, dt_bias, a_log, d_skip, norm_ffn, w_gate, w_up, w_down):
    wi = w_in[l]
    w_kr = wi[:, S_Q + KV_LORA:]
    misc = jnp.concatenate([
        wi[:, S_XBC:S_DT], jnp.zeros((D_MODEL, MISC_KR - SSD_HEADS), F32),
        w_kr, _swap_halves_cols(w_kr),
        jnp.zeros((D_MODEL, LANES - MISC_KR_SWAPPED - QK_ROPE), F32)], axis=1)
    w_in_r = jnp.concatenate([wi[:, :S_XBC], wi[:, S_DT:S_Q], wi[:, S_Q:S_Q + KV_LORA], misc], axis=1)

    wq = w_uq[l].reshape(Q_LORA, MLA_HEADS, QK_NOPE + QK_ROPE)
    wq_rope = wq[:, :, QK_NOPE:]
    w_q = jnp.concatenate([wq, _swap_halves_cols(wq_rope)], axis=-1).reshape(Q_LORA, MLA_HEADS * LANES)

    uk_t = jnp.transpose(w_uk[l], (1, 2, 0))
    eye = jnp.eye(QK_ROPE, dtype=F32)
    top = jnp.concatenate([uk_t, jnp.zeros((MLA_HEADS, QK_NOPE, KEY_WIDTH - KV_LORA), F32)], axis=2)
    mid = jnp.concatenate([jnp.zeros((QK_ROPE, KV_LORA), F32), eye,
                           jnp.zeros((QK_ROPE, KEY_WIDTH - KV_LORA - QK_ROPE), F32)], axis=1)
    mid = jnp.broadcast_to(mid[None], (MLA_HEADS, QK_ROPE, KEY_WIDTH))
    bot = jnp.zeros((MLA_HEADS, LANES - QK_NOPE - QK_ROPE, KEY_WIDTH), F32)
    w_uk_p = jnp.concatenate([top, mid, bot], axis=1)

    uv = jnp.transpose(w_uv[l], (1, 0, 2)).reshape(MLA_HEADS // 2, 2, KV_LORA, V_HEAD)
    zero = jnp.zeros((MLA_HEADS // 2, KV_LORA, V_HEAD), F32)
    w_uv_p = jnp.concatenate([jnp.concatenate([uv[:, 0], zero], axis=2),
                              jnp.concatenate([zero, uv[:, 1]], axis=2)], axis=1)

    pad_heads = lambda v: jnp.concatenate([v, jnp.zeros((LANES - SSD_HEADS,), F32)])[None]
    return {
        "w_in": w_in_r.astype(BF16), "w_q": w_q.astype(BF16), "w_uk": w_uk_p.astype(BF16),
        "w_uv": w_uv_p.astype(BF16),
        "w_out_a": w_out[l][:SSD_WIDTH].astype(BF16), "w_out_b": w_out[l][SSD_WIDTH:].astype(BF16),
        "w_gate": w_gate[l].astype(BF16), "w_up": w_up[l].astype(BF16), "w_down": w_down[l].astype(BF16),
        "norm_mix": norm_mix[l][None], "q_norm": q_norm[l][None], "kv_norm": kv_norm[l][None],
        "ssd_norm": ssd_norm[l][None], "norm_ffn": norm_ffn[l][None],
        "conv_w": conv_w[l], "conv_b": conv_b[l][None],
        "dt_bias": pad_heads(dt_bias[l]), "a_log": pad_heads(a_log[l]),
        "d_skip": jnp.repeat(d_skip[l], SSD_HEAD_DIM)[None],
    }


def _rope_tables(past, T):
    pos = (past + jnp.arange(T)).astype(F32)
    inv = 1.0 / (ROPE_BASE ** (jnp.arange(0, QK_ROPE, 2, dtype=F32) / QK_ROPE))
    ang = pos[:, None] * inv[None, :]
    cos, sin = jnp.cos(ang), jnp.sin(ang)
    c32 = jnp.concatenate([cos, cos], axis=1)
    s32 = jnp.concatenate([-sin, sin], axis=1)
    scale = (QK_NOPE + QK_ROPE) ** -0.5 * LOG2_E
    zeros = lambda w: jnp.zeros((T, w), F32)
    return {
        "ck": jnp.concatenate([c32, zeros(LANES - QK_ROPE)], axis=1),
        "sk": jnp.concatenate([s32, zeros(LANES - QK_ROPE)], axis=1),
        "cq": scale * jnp.concatenate([jnp.ones((T, QK_NOPE), F32), c32, zeros(LANES - QK_NOPE - QK_ROPE)], axis=1),
        "sq": scale * jnp.concatenate([zeros(QK_NOPE), s32, zeros(LANES - QK_NOPE - QK_ROPE)], axis=1),
    }


def _hybrid_layer(x2d, B, T, past_keys, h0, conv0, lp, tabs, norm_final, final):
    past = 0 if past_keys is None else past_keys.shape[1]
    z, xbc, cqn, ckv, kr, keys, dt = _in_proj(x2d, T, lp, tabs)
    y, h_new, conv_new = _ssd(xbc, z, dt, h0, conv0, B, T, lp)
    y = y.reshape(B * T, SSD_WIDTH)
    keys3d = keys.reshape(B, T, KEY_WIDTH)
    if past_keys is not None:
        keys3d = jnp.concatenate([past_keys, keys3d], axis=1)
    o = _attn(cqn, keys3d, B, T, past, lp, tabs)
    x_new = _ffn(x2d, y, o, lp, norm_final, final)
    return x_new, ckv, kr, h_new, conv_new


def kernel(x_prompt, x_sample, cache_mla_ckv, cache_mla_krope, state_ssm, state_conv, w_in, w_uq, w_uk, w_uv, w_out, norm_mix, q_norm, kv_norm, ssd_norm, conv_w, conv_b, dt_bias, a_log, d_skip, norm_ffn, w_gate, w_up, w_down, norm_final):
    depth = w_in.shape[0]
    bp, tp, _ = x_prompt.shape
    bs, tsm, _ = x_sample.shape
    past = cache_mla_ckv.shape[2]
    state_rows = SSD_HEADS * SSD_HEAD_DIM

    xp = x_prompt.reshape(bp * tp, D_MODEL)
    xs = x_sample.reshape(bs * tsm, D_MODEL)
    tabs_p = _rope_tables(0, tp)
    tabs_s = _rope_tables(past, tsm)
    zero_state = jnp.zeros((bp, state_rows, D_STATE), F32)
    zero_conv = jnp.zeros((bp, CONV_W - 1, CONV_DIM), F32)
    nfin = norm_final[None]

    outs_p = [[], [], [], []]
    outs_s = [[], [], [], []]
    for l in range(depth):
        lp = _layer_params(l, w_in, w_uq, w_uk, w_uv, w_out, norm_mix, q_norm, kv_norm, ssd_norm,
                           conv_w, conv_b, dt_bias, a_log, d_skip, norm_ffn, w_gate, w_up, w_down)
        final = l == depth - 1
        xp, ckv, kr, hs, cb = _hybrid_layer(xp, bp, tp, None, zero_state, zero_conv, lp, tabs_p, nfin, final)
        for acc, v in zip(outs_p, (ckv.reshape(bp, tp, KV_LORA), kr.reshape(bp, tp, QK_ROPE),
                                   hs.reshape(bp, SSD_HEADS, SSD_HEAD_DIM, D_STATE), cb)):
            acc.append(v)
        past_keys = jnp.concatenate(
            [cache_mla_ckv[l], cache_mla_krope[l],
             jnp.zeros((bs, past, KEY_WIDTH - KV_LORA - QK_ROPE), F32)], axis=-1).astype(BF16)
        xs, ckv, kr, hs, cb = _hybrid_layer(
            xs, bs, tsm, past_keys, state_ssm[l].reshape(bs, state_rows, D_STATE), state_conv[l],
            lp, tabs_s, nfin, final)
        for acc, v in zip(outs_s, (ckv.reshape(bs, tsm, KV_LORA), kr.reshape(bs, tsm, QK_ROPE),
                                   hs.reshape(bs, SSD_HEADS, SSD_HEAD_DIM, D_STATE), cb)):
            acc.append(v)
    return (xp.reshape(bp, tp, D_MODEL), xs.reshape(bs, tsm, D_MODEL),
            *(jnp.stack(v) for v in outs_p), *(jnp.stack(v) for v in outs_s))
```
